```python
import jax, jax.numpy as jnp
from jax import lax
import numpy as np

D_MODEL = 1024
BATCH = 2
SEQ = 8192
DEPTH = 4
DEC_BATCH = 8
DEC_SEQ = 2048
PAST_LEN = 128

N_HEADS_A = 8
QK_NOPE = 64
QK_ROPE = 32
QK_DIM = QK_NOPE + QK_ROPE
V_DIM = 64
Q_LORA = 384
KV_LORA = 256
ROPE_BASE = 10000.0
Q_BLOCK = 128
CHUNK = 128
SGU_GROUPS = 8
GMLP_HALF = D_MODEL
SGU_CH = GMLP_HALF // SGU_GROUPS
D_FF = -(-8 * D_MODEL // (3 * 256)) * 256
IN_SPLITS = (Q_LORA, KV_LORA, QK_ROPE, GMLP_HALF, GMLP_HALF, D_MODEL, D_MODEL)
IN_COLS = sum(IN_SPLITS)
IN_OFFSETS = tuple(int(o) for o in np.cumsum(IN_SPLITS)[:-1])
N_MOD = 6
EPS = 1e-6

kernel_name = "hybrid_mla_sgu_adaln_encoder"


def rms_norm(x, g):
    x32 = x.astype(jnp.float32)
    y = x32 * lax.rsqrt(jnp.mean(x32 * x32, axis=-1, keepdims=True) + EPS)
    return y.astype(x.dtype) * g


def rope_tables(seq_len, dtype):
    pos = jnp.arange(seq_len, dtype=jnp.float32)
    inv = ROPE_BASE ** (-jnp.arange(0, QK_ROPE, 2, dtype=jnp.float32) / QK_ROPE)
    ang = pos[:, None] * inv[None, :]
    return jnp.cos(ang).astype(dtype), jnp.sin(ang).astype(dtype)


def apply_rope(x, cos, sin):
    x1, x2 = jnp.split(x, 2, axis=-1)
    c = cos[None, :, None, :]
    s = sin[None, :, None, :]
    return jnp.concatenate([x1 * c - x2 * s, x2 * c + x1 * s], axis=-1)


def mla_branch(q_lat, kv_lat, k_rope_raw, p, cos, sin):
    B, S, _ = q_lat.shape
    q = (rms_norm(q_lat, p["q_a_norm_g"]) @ p["w_q_b"]).reshape(B, S, N_HEADS_A, QK_DIM)
    kv = (rms_norm(kv_lat, p["kv_a_norm_g"]) @ p["w_kv_b"]).reshape(B, S, N_HEADS_A, QK_NOPE + V_DIM)
    k_nope, v = kv[..., :QK_NOPE], kv[..., QK_NOPE:]
    k_rope = jnp.broadcast_to(k_rope_raw[:, :, None, :], (B, S, N_HEADS_A, QK_ROPE))
    k = jnp.concatenate([k_nope, k_rope], axis=-1)
    q = rms_norm(q, p["q_norm_g"])
    k = rms_norm(k, p["k_norm_g"])
    q = jnp.concatenate([q[..., :QK_NOPE], apply_rope(q[..., QK_NOPE:], cos, sin)], axis=-1)
    k = jnp.concatenate([k[..., :QK_NOPE], apply_rope(k[..., QK_NOPE:], cos, sin)], axis=-1)
    nb = S // Q_BLOCK
    qb = q.reshape(B, nb, Q_BLOCK, N_HEADS_A, QK_DIM).transpose(1, 0, 2, 3, 4)
    scale = QK_DIM ** -0.5

    def attend(q_blk):
        s = jnp.einsum('bqhd,bkhd->bhqk', q_blk, k).astype(jnp.float32) * scale
        pr = jax.nn.softmax(s, axis=-1).astype(v.dtype)
        return jnp.einsum('bhqk,bkhd->bqhd', pr, v)

    o = lax.map(attend, qb)
    o = o.transpose(1, 0, 2, 3, 4).reshape(B, S, N_HEADS_A * V_DIM)
    return o @ p["w_o_a"]


def sgu_branch(zu, zv, p):
    B, S, _ = zu.shape
    u = jax.nn.gelu(zu)
    v = rms_norm(jax.nn.gelu(zv), p["sgu_norm_g"])
    v5 = v.reshape(B, S // CHUNK, CHUNK, SGU_GROUPS, SGU_CH)
    mixed = jnp.einsum('gpq,bnqgc->bnpgc', p["w_s"], v5) + p["b_s"].T[None, None, :, :, None]
    return (u * mixed.reshape(B, S, GMLP_HALF)) @ p["w_o_b"]


def encoder_layer(x, c, p, cos, sin):
    mod = jax.nn.silu(c) @ p["w_ada"] + p["b_ada"]
    sh1, sc1, g1, sh2, sc2, g2 = [m[:, None, :] for m in jnp.split(mod, N_MOD, axis=-1)]
    h = rms_norm(x, p["norm1_g"]) * (1 + sc1) + sh1
    proj = h @ p["w_in"]
    q_lat, kv_lat, k_rope, zu, zv, ga, gb = jnp.split(proj, IN_OFFSETS, axis=-1)
    out_a = mla_branch(q_lat, kv_lat, k_rope, p, cos, sin)
    out_b = sgu_branch(zu, zv, p)
    merged = jax.nn.sigmoid(ga) * out_a + jax.nn.sigmoid(gb) * out_b
    x = x + g1 * (merged @ p["w_out"])
    h2 = rms_norm(x, p["norm2_g"]) * (1 + sc2) + sh2
    up, gate = jnp.split(h2 @ p["w_ffn_in"], 2, axis=-1)
    x = x + g2 * ((jax.nn.silu(gate) * up) @ p["w_ffn_out"])
    return x


def setup_inputs(seed: int = 0) -> dict:
    key = jax.random.key(seed)
    ks = jax.random.split(key, 24)
    f32 = jnp.float32

    def nrm(k, shape, s):
        return jax.random.normal(k, shape, f32) * s

    def gain(k, shape):
        return 1.0 + 0.05 * jax.random.normal(k, shape, f32)

    L = DEPTH
    return {
        "x_prompt": nrm(ks[0], (BATCH, SEQ, D_MODEL), 1.0),
        "x_sample": nrm(ks[1], (DEC_BATCH, DEC_SEQ, D_MODEL), 1.0),
        "c_prompt": nrm(ks[2], (BATCH, D_MODEL), 1.0),
        "c_sample": nrm(ks[3], (DEC_BATCH, D_MODEL), 1.0),
        "w_ada": nrm(ks[4], (L, D_MODEL, N_MOD * D_MODEL), 0.5 * D_MODEL ** -0.5),
        "b_ada": nrm(ks[5], (L, N_MOD * D_MODEL), 0.02),
        "norm1_g": gain(ks[6], (L, D_MODEL)),
        "w_in": nrm(ks[7], (L, D_MODEL, IN_COLS), D_MODEL ** -0.5),
        "q_a_norm_g": gain(ks[8], (L, Q_LORA)),
        "kv_a_norm_g": gain(ks[9], (L, KV_LORA)),
        "w_q_b": nrm(ks[10], (L, Q_LORA, N_HEADS_A * QK_DIM), Q_LORA ** -0.5),
        "w_kv_b": nrm(ks[11], (L, KV_LORA, N_HEADS_A * (QK_NOPE + V_DIM)), KV_LORA ** -0.5),
        "q_norm_g": gain(ks[12], (L, QK_DIM)),
        "k_norm_g": gain(ks[13], (L, QK_DIM)),
        "w_o_a": nrm(ks[14], (L, N_HEADS_A * V_DIM, D_MODEL), (N_HEADS_A * V_DIM) ** -0.5),
        "sgu_norm_g": gain(ks[15], (L, GMLP_HALF)),
        "w_s": nrm(ks[16], (L, SGU_GROUPS, CHUNK, CHUNK), CHUNK ** -0.5),
        "b_s": 1.0 + nrm(ks[17], (L, SGU_GROUPS, CHUNK), 0.1),
        "w_o_b": nrm(ks[18], (L, GMLP_HALF, D_MODEL), GMLP_HALF ** -0.5),
        "w_out": nrm(ks[19], (L, D_MODEL, D_MODEL), D_MODEL ** -0.5),
        "norm2_g": gain(ks[20], (L, D_MODEL)),
        "w_ffn_in": nrm(ks[21], (L, D_MODEL, 2 * D_FF), D_MODEL ** -0.5),
        "w_ffn_out": nrm(ks[22], (L, D_FF, D_MODEL), D_FF ** -0.5),
    }


def reference(x_prompt, x_sample, c_prompt, c_sample, w_ada, b_ada, norm1_g, w_in,
              q_a_norm_g, kv_a_norm_g, w_q_b, w_kv_b, q_norm_g, k_norm_g, w_o_a,
              sgu_norm_g, w_s, b_s, w_o_b, w_out, norm2_g, w_ffn_in, w_ffn_out):
    cos_p, sin_p = rope_tables(x_prompt.shape[1], x_prompt.dtype)
    cos_s, sin_s = rope_tables(x_sample.shape[1], x_sample.dtype)
    y_prompt, y_sample = x_prompt, x_sample
    for l in range(DEPTH):
        p = {
            "w_ada": w_ada[l], "b_ada": b_ada[l], "norm1_g": norm1_g[l], "w_in": w_in[l],
            "q_a_norm_g": q_a_norm_g[l], "kv_a_norm_g": kv_a_norm_g[l],
            "w_q_b": w_q_b[l], "w_kv_b": w_kv_b[l], "q_norm_g": q_norm_g[l], "k_norm_g": k_norm_g[l],
            "w_o_a": w_o_a[l], "sgu_norm_g": sgu_norm_g[l], "w_s": w_s[l], "b_s": b_s[l],
            "w_o_b": w_o_b[l], "w_out": w_out[l], "norm2_g": norm2_g[l],
            "w_ffn_in": w_ffn_in[l], "w_ffn_out": w_ffn_out[l],
        }
        y_prompt = encoder_layer(y_prompt, c_prompt, p, cos_p, sin_p)
        y_sample = encoder_layer(y_sample, c_sample, p, cos_s, sin_s)
    return (y_prompt, y_sample)
```

```python
import functools
import math

import jax
import jax.numpy as jnp
import numpy as np
from jax import lax
from jax.experimental import pallas as pl
from jax.experimental.pallas import tpu as pltpu

D_MODEL = 1024
DEPTH = 4
N_HEADS = 8
QK_NOPE = 64
QK_ROPE = 32
QK_DIM = QK_NOPE + QK_ROPE
V_DIM = 64
Q_LORA = 384
KV_LORA = 256
ROPE_BASE = 10000.0
CHUNK = 128
SGU_GROUPS = 8
D_FF = 2816
N_MOD = 6
EPS = 1e-6

LANES = 128
SLOTS = N_HEADS * LANES
LAT_COLS = 768
MOD_ROWS = 16
VMEM_LIMIT = 56 * 1024 * 1024

TM_IN = 256
TM_OUT = 256
TQ = 512
TK = 512
MOD_TN = 512

F32 = jnp.float32
BF16 = jnp.bfloat16


def _dot(a, b):
    return jnp.dot(a, b, preferred_element_type=F32)


def _rms_scale(x, n):
    return lax.rsqrt(jnp.sum(x * x, axis=-1, keepdims=True) * (1.0 / n) + EPS)


def _gelu_tanh(x):
    c = math.sqrt(2.0 / math.pi)
    return 0.5 * x * (1.0 + jnp.tanh(c * (x + 0.044715 * (x * x * x))))


def _sigmoid(x):
    return 1.0 / (1.0 + jnp.exp(-x))


def _mod_kernel(c_ref, w_ref, b_ref, o_ref):
    c = c_ref[...]
    a = (c * _sigmoid(c)).astype(BF16)
    o_ref[...] = _dot(a, w_ref[...].astype(BF16)) + b_ref[...]


def _modulation(c16, w_ada, b_ada):
    n_cols = N_MOD * D_MODEL
    return pl.pallas_call(
        _mod_kernel,
        out_shape=jax.ShapeDtypeStruct((DEPTH, MOD_ROWS, n_cols), F32),
        grid=(DEPTH, n_cols // MOD_TN),
        in_specs=[
            pl.BlockSpec((MOD_ROWS, D_MODEL), lambda l, j: (0, 0)),
            pl.BlockSpec((None, D_MODEL, MOD_TN), lambda l, j: (l, 0, j)),
            pl.BlockSpec((None, 1, MOD_TN), lambda l, j: (l, 0, j)),
        ],
        out_specs=pl.BlockSpec((None, MOD_ROWS, MOD_TN), lambda l, j: (l, 0, j)),
        compiler_params=pltpu.CompilerParams(
            dimension_semantics=("arbitrary", "arbitrary"),
            vmem_limit_bytes=VMEM_LIMIT),
        name="adaln_modulation",
    )(c16, w_ada, b_ada.reshape(DEPTH, 1, n_cols))


def _rope(x, c, s1, s2):
    return x * c + pltpu.roll(x, 16, axis=1) * s1 + pltpu.roll(x, LANES - 16, axis=1) * s2


def _mixer_in_kernel(x_ref, sh_ref, sc_ref, n1g_ref, wlat_ref, wbig_ref, gqa_ref, gkva_ref,
                     wqb_ref, wkvb_ref, gq_ref, gk_ref, vone_ref, tc_ref, ts1_ref, ts2_ref,
                     gsgu_ref, ws_ref, bs_ref, wob_ref,
                     q_ref, k_ref, v_ref, sga_ref, pb_ref, prod_scr):
    tm = x_ref.shape[0]
    x = x_ref[...]
    h = x * _rms_scale(x, D_MODEL) * n1g_ref[...]
    h = h * (1.0 + sc_ref[...]) + sh_ref[...]
    hb = h.astype(BF16)

    lat = _dot(hb, wlat_ref[...])
    q_lat = lat[:, :Q_LORA]
    kv_lat = lat[:, Q_LORA:Q_LORA + KV_LORA]
    kr_slot = pltpu.roll(lat[:, Q_LORA + KV_LORA:], QK_NOPE, axis=1)
    qn = (q_lat * _rms_scale(q_lat, Q_LORA) * gqa_ref[...]).astype(BF16)
    kvn = (kv_lat * _rms_scale(kv_lat, KV_LORA) * gkva_ref[...]).astype(BF16)
    q_all = _dot(qn, wqb_ref[...])
    kv_all = _dot(kvn, wkvb_ref[...])
    v_ref[...] = (kv_all[:, SLOTS:] + vone_ref[...]).astype(BF16)

    tc, ts1, ts2 = tc_ref[...], ts1_ref[...], ts2_ref[...]
    gq, gk = gq_ref[...], gk_ref[...]
    for hd in range(N_HEADS):
        sl = slice(hd * LANES, (hd + 1) * LANES)
        qh = q_all[:, sl]
        qh = qh * _rms_scale(qh, QK_DIM) * gq
        q_ref[:, sl] = _rope(qh, tc, ts1, ts2).astype(BF16)
        kh = kv_all[:, sl] + kr_slot
        kh = kh * _rms_scale(kh, QK_DIM) * gk
        k_ref[:, sl] = _rope(kh, tc, ts1, ts2).astype(BF16)

    u = _gelu_tanh(_dot(hb, wbig_ref[:, 0:D_MODEL]))
    gv = _gelu_tanh(_dot(hb, wbig_ref[:, D_MODEL:2 * D_MODEL]))
    vn = (gv * _rms_scale(gv, D_MODEL) * gsgu_ref[...]).astype(BF16)
    for c in range(tm // CHUNK):
        rows = slice(c * CHUNK, (c + 1) * CHUNK)
        for g in range(SGU_GROUPS):
            cols = slice(g * CHUNK, (g + 1) * CHUNK)
            mixed = _dot(ws_ref[g], vn[rows, cols]) + bs_ref[:, cols]
            prod_scr[rows, cols] = (u[rows, cols] * mixed).astype(BF16)
    out_b = _dot(prod_scr[...], wob_ref[...])

    sga_ref[...] = _sigmoid(_dot(hb, wbig_ref[:, 2 * D_MODEL:3 * D_MODEL]))
    pb_ref[...] = _sigmoid(_dot(hb, wbig_ref[:, 3 * D_MODEL:4 * D_MODEL])) * out_b


def _const_spec(shape, layer=None):
    if layer is None:
        return pl.BlockSpec(shape, lambda i: (0,) * len(shape), pipeline_mode=pl.Buffered(1))
    return pl.BlockSpec((None,) + shape, lambda i: (layer,) + (0,) * len(shape),
                        pipeline_mode=pl.Buffered(1))


def _mod_spec(layer, which, row0, tiles_per_seq):
    return pl.BlockSpec((None, None, None, 1, D_MODEL),
                        lambda i: (layer, row0 + i // tiles_per_seq, which, 0, 0))


def _mixer_in(layer, x2d, mod5, row0, seq, w, tabs):
    n_tok = x2d.shape[0]
    tm = TM_IN
    tps = seq // tm
    tile = lambda cols: pl.BlockSpec((tm, cols), lambda i: (i, 0))
    tab_spec = pl.BlockSpec((tm, LANES), lambda i: (i % tps, 0))
    in_specs = [
        tile(D_MODEL),
        _mod_spec(layer, 0, row0, tps),
        _mod_spec(layer, 1, row0, tps),
        _const_spec((1, D_MODEL), layer),
        _const_spec((D_MODEL, LAT_COLS), layer),
        _const_spec((D_MODEL, 4 * D_MODEL), layer),
        _const_spec((1, Q_LORA), layer),
        _const_spec((1, KV_LORA), layer),
        _const_spec((Q_LORA, SLOTS), layer),
        _const_spec((KV_LORA, 2 * SLOTS), layer),
        _const_spec((1, LANES), layer),
        _const_spec((1, LANES), layer),
        _const_spec((1, SLOTS)),
        tab_spec, tab_spec, tab_spec,
        _const_spec((1, D_MODEL), layer),
        _const_spec((SGU_GROUPS, CHUNK, CHUNK), layer),
        _const_spec((CHUNK, D_MODEL), layer),
        _const_spec((D_MODEL, D_MODEL), layer),
    ]
    out_shape = (
        jax.ShapeDtypeStruct((n_tok, SLOTS), BF16),
        jax.ShapeDtypeStruct((n_tok, SLOTS), BF16),
        jax.ShapeDtypeStruct((n_tok, SLOTS), BF16),
        jax.ShapeDtypeStruct((n_tok, D_MODEL), F32),
        jax.ShapeDtypeStruct((n_tok, D_MODEL), F32),
    )
    out_specs = (tile(SLOTS), tile(SLOTS), tile(SLOTS), tile(D_MODEL), tile(D_MODEL))
    return pl.pallas_call(
        _mixer_in_kernel,
        out_shape=out_shape,
        grid=(n_tok // tm,),
        in_specs=in_specs,
        out_specs=out_specs,
        scratch_shapes=[pltpu.VMEM((tm, D_MODEL), BF16)],
        compiler_params=pltpu.CompilerParams(
            dimension_semantics=("arbitrary",), vmem_limit_bytes=VMEM_LIMIT),
        name="mixer_in",
    )(x2d, mod5, mod5, w["norm1_g"], w["w_lat"], w["w_big"], w["q_a_norm_g"], w["kv_a_norm_g"],
      w["w_q_b"], w["w_kv_b"], w["gq_slot"], w["gk_slot"], w["v_ones"],
      tabs[0], tabs[1], tabs[2], w["sgu_norm_g"], w["w_s"], w["b_s_full"], w["w_o_b"])


def _attn_kernel(q_ref, k_ref, v_ref, o_ref):
    tq = q_ref.shape[0]
    seq = k_ref.shape[0]
    lane = lax.broadcasted_iota(jnp.int32, (tq, LANES), 1)
    outs = []
    for hh in range(2):
        sl = slice(hh * LANES, (hh + 1) * LANES)
        q = q_ref[:, sl]

        def body(j, carry, sl=sl, q=q):
            m, acc = carry
            start = pl.multiple_of(j * TK, TK)
            kc = k_ref[pl.ds(start, TK), sl]
            vc = v_ref[pl.ds(start, TK), sl]
            s = lax.dot_general(q, kc, (((1,), (1,)), ((), ())), preferred_element_type=F32)
            m_new = jnp.maximum(m, jnp.max(s, axis=-1, keepdims=True))
            p = jnp.exp2(s - m_new).astype(BF16)
            acc = jnp.exp2(m - m_new) * acc + _dot(p, vc)
            return m_new, acc

        m0 = jnp.full((tq, 1), -jnp.inf, F32)
        acc0 = jnp.zeros((tq, LANES), F32)
        _, acc = lax.fori_loop(0, seq // TK, body, (m0, acc0))
        den_lane = V_DIM if hh == 0 else 0
        den = acc[:, den_lane:den_lane + 1]
        outs.append(acc * (1.0 / den))
    o_ref[...] = jnp.where(lane < V_DIM, outs[0], outs[1]).astype(BF16)


def _attention(q, k, v, batch, seq):
    n_tok = q.shape[0]
    qt = seq // TQ
    pair = 2 * LANES
    return pl.pallas_call(
        _attn_kernel,
        out_shape=jax.ShapeDtypeStruct((n_tok, N_HEADS * V_DIM), BF16),
        grid=(batch, N_HEADS // 2, qt),
        in_specs=[
            pl.BlockSpec((TQ, pair), lambda b, hp, i: (b * qt + i, hp)),
            pl.BlockSpec((seq, pair), lambda b, hp, i: (b, hp)),
            pl.BlockSpec((seq, pair), lambda b, hp, i: (b, hp)),
        ],
        out_specs=pl.BlockSpec((TQ, LANES), lambda b, hp, i: (b * qt + i, hp)),
        compiler_params=pltpu.CompilerParams(
            dimension_semantics=("arbitrary", "arbitrary", "arbitrary"),
            vmem_limit_bytes=VMEM_LIMIT),
        name="attention",
    )(q, k, v)


def _mixer_out_kernel(x_ref, o_ref, sga_ref, pb_ref, g1_ref, sh2_ref, sc2_ref, g2_ref,
                      woa_ref, wout_ref, n2g_ref, wfi_ref, wfo_ref, y_ref):
    out_a = _dot(o_ref[...], woa_ref[...])
    merged = (sga_ref[...] * out_a + pb_ref[...]).astype(BF16)
    x1 = x_ref[...] + g1_ref[...] * _dot(merged, wout_ref[...])
    h2 = x1 * _rms_scale(x1, D_MODEL) * n2g_ref[...]
    h2 = (h2 * (1.0 + sc2_ref[...]) + sh2_ref[...]).astype(BF16)
    up = _dot(h2, wfi_ref[:, :D_FF])
    gate = _dot(h2, wfi_ref[:, D_FF:])
    act = (gate * _sigmoid(gate) * up).astype(BF16)
    y_ref[...] = x1 + g2_ref[...] * _dot(act, wfo_ref[...])


def _mixer_out(layer, x2d, o, sga, pb, mod5, row0, seq, w):
    n_tok = x2d.shape[0]
    tm = TM_OUT
    tps = seq // tm
    tile = lambda cols: pl.BlockSpec((tm, cols), lambda i: (i, 0))
    in_specs = [
        tile(D_MODEL), tile(N_HEADS * V_DIM), tile(D_MODEL), tile(D_MODEL),
        _mod_spec(layer, 2, row0, tps),
        _mod_spec(layer, 3, row0, tps),
        _mod_spec(layer, 4, row0, tps),
        _mod_spec(layer, 5, row0, tps),
        _const_spec((N_HEADS * V_DIM, D_MODEL), layer),
        _const_spec((D_MODEL, D_MODEL), layer),
        _const_spec((1, D_MODEL), layer),
        _const_spec((D_MODEL, 2 * D_FF), layer),
        _const_spec((D_FF, D_MODEL), layer),
    ]
    return pl.pallas_call(
        _mixer_out_kernel,
        out_shape=jax.ShapeDtypeStruct((n_tok, D_MODEL), F32),
        grid=(n_tok // tm,),
        in_specs=in_specs,
        out_specs=tile(D_MODEL),
        compiler_params=pltpu.CompilerParams(
            dimension_semantics=("arbitrary",), vmem_limit_bytes=VMEM_LIMIT),
        name="mixer_out",
    )(x2d, o, sga, pb, mod5, mod5, mod5, mod5,
      w["w_o_a"], w["w_out"], w["norm2_g"], w["w_ffn_in"], w["w_ffn_out"])


def _head_slots(w, width):
    L, K, _ = w.shape
    w = w.reshape(L, K, N_HEADS, width)
    w = jnp.pad(w, ((0, 0), (0, 0), (0, 0), (0, LANES - width)))
    return w.reshape(L, K, SLOTS)


def _rope_tables(seq):
    pos = jnp.arange(seq, dtype=F32)
    inv = ROPE_BASE ** (-jnp.arange(0, QK_ROPE, 2, dtype=F32) / QK_ROPE)
    ang = pos[:, None] * inv[None, :]
    cos, sin = jnp.cos(ang), jnp.sin(ang)
    half = QK_ROPE // 2
    ones = jnp.ones((seq, QK_NOPE), F32)
    zpad = jnp.zeros((seq, LANES - QK_DIM), F32)
    zh = jnp.zeros((seq, half), F32)
    zn = jnp.zeros((seq, QK_NOPE), F32)
    tc = jnp.concatenate([ones, cos, cos, zpad], axis=1)
    ts1 = jnp.concatenate([zn, zh, sin, zpad], axis=1)
    ts2 = jnp.concatenate([zn, -sin, zh, zpad], axis=1)
    return tc, ts1, ts2


def _prepare_weights(norm1_g, w_in, q_a_norm_g, kv_a_norm_g, w_q_b, w_kv_b, q_norm_g, k_norm_g,
                     w_o_a, sgu_norm_g, w_s, b_s, w_o_b, w_out, norm2_g, w_ffn_in, w_ffn_out):
    L = DEPTH
    lat_end = Q_LORA + KV_LORA + QK_ROPE
    w_lat = jnp.pad(w_in[:, :, :lat_end], ((0, 0), (0, 0), (0, LAT_COLS - lat_end))).astype(BF16)
    w_big = w_in[:, :, lat_end:].astype(BF16)
    kv4 = w_kv_b.reshape(L, KV_LORA, N_HEADS, QK_NOPE + V_DIM)
    w_kn = _head_slots(kv4[..., :QK_NOPE].reshape(L, KV_LORA, N_HEADS * QK_NOPE), QK_NOPE)
    wv = kv4[..., QK_NOPE:]
    zeros = jnp.zeros_like(wv)
    even = jnp.concatenate([wv, zeros], axis=-1)
    odd = jnp.concatenate([zeros, wv], axis=-1)
    is_even = (jnp.arange(N_HEADS) % 2 == 0)[None, None, :, None]
    w_v = jnp.where(is_even, even, odd).reshape(L, KV_LORA, SLOTS)
    v_ones = np.zeros((1, SLOTS), np.float32)
    for hd in range(N_HEADS):
        v_ones[0, hd * LANES + (V_DIM if hd % 2 == 0 else 0)] = 1.0
    q_scale = (QK_DIM ** -0.5) * math.log2(math.e)
    pad_gain = lambda g: jnp.pad(g, ((0, 0), (0, LANES - QK_DIM))).reshape(L, 1, LANES)
    return {
        "norm1_g": norm1_g.reshape(L, 1, D_MODEL),
        "w_lat": w_lat,
        "w_big": w_big,
        "q_a_norm_g": q_a_norm_g.reshape(L, 1, Q_LORA),
        "kv_a_norm_g": kv_a_norm_g.reshape(L, 1, KV_LORA),
        "w_q_b": _head_slots(w_q_b, QK_DIM).astype(BF16),
        "w_kv_b": jnp.concatenate([w_kn, w_v], axis=-1).astype(BF16),
        "gq_slot": pad_gain(q_norm_g * q_scale),
        "gk_slot": pad_gain(k_norm_g),
        "v_ones": jnp.asarray(v_ones),
        "sgu_norm_g": sgu_norm_g.reshape(L, 1, D_MODEL),
        "w_s": w_s.astype(BF16),
        "b_s_full": jnp.repeat(jnp.swapaxes(b_s, 1, 2), CHUNK, axis=2),
        "w_o_b": w_o_b.astype(BF16),
        "w_o_a": w_o_a.astype(BF16),
        "w_out": w_out.astype(BF16),
        "norm2_g": norm2_g.reshape(L, 1, D_MODEL),
        "w_ffn_in": w_ffn_in.astype(BF16),
        "w_ffn_out": w_ffn_out.astype(BF16),
    }


def kernel(x_prompt, x_sample, c_prompt, c_sample, w_ada, b_ada, norm1_g, w_in, q_a_norm_g,
           kv_a_norm_g, w_q_b, w_kv_b, q_norm_g, k_norm_g, w_o_a, sgu_norm_g, w_s, b_s, w_o_b,
           w_out, norm2_g, w_ffn_in, w_ffn_out):
    w = _prepare_weights(norm1_g, w_in, q_a_norm_g, kv_a_norm_g, w_q_b, w_kv_b, q_norm_g,
                         k_norm_g, w_o_a, sgu_norm_g, w_s, b_s, w_o_b, w_out, norm2_g,
                         w_ffn_in, w_ffn_out)
    n_prompt, n_sample = c_prompt.shape[0], c_sample.shape[0]
    c16 = jnp.concatenate(
        [c_prompt, c_sample, jnp.zeros((MOD_ROWS - n_prompt - n_sample, D_MODEL), F32)], axis=0)
    mod = _modulation(c16, w_ada, b_ada)
    mod5 = mod.reshape(DEPTH, MOD_ROWS, N_MOD, 1, D_MODEL)

    groups = []
    for x, row0 in ((x_prompt, 0), (x_sample, n_prompt)):
        batch, seq, _ = x.shape
        groups.append([x.reshape(batch * seq, D_MODEL), row0, batch, seq, _rope_tables(seq)])

    for layer in range(DEPTH):
        for grp in groups:
            x2d, row0, batch, seq, tabs = grp
            q, k, v, sga, pb = _mixer_in(layer, x2d, mod5, row0, seq, w, tabs)
            o = _attention(q, k, v, batch, seq)
            grp[0] = _mixer_out(layer, x2d, o, sga, pb, mod5, row0, seq, w)

    return tuple(grp[0].reshape(grp[2], grp[3], D_MODEL) for grp in groups)
```

```python
import functools
import math

import jax
import jax.numpy as jnp
import numpy as np
from jax import lax
from jax.experimental import pallas as pl
from jax.experimental.pallas import tpu as pltpu

D_MODEL = 1024
DEPTH = 4
N_HEADS = 8
QK_NOPE = 64
QK_ROPE = 32
QK_DIM = QK_NOPE + QK_ROPE
V_DIM = 64
Q_LORA = 384
KV_LORA = 256
ROPE_BASE = 10000.0
CHUNK = 128
SGU_GROUPS = 8
D_FF = 2816
N_MOD = 6
EPS = 1e-6

LANES = 128
SLOTS = N_HEADS * LANES
LAT_COLS = 768
MOD_ROWS = 16
VMEM_LIMIT = 56 * 1024 * 1024

TM_IN = 256
TM_OUT = 256
TQ = 1024
TK = TM_IN
MOD_TN = 512

F32 = jnp.float32
BF16 = jnp.bfloat16


def _dot(a, b):
    return jnp.dot(a, b, preferred_element_type=F32)


def _rms_scale(x, n):
    return lax.rsqrt(jnp.sum(x * x, axis=-1, keepdims=True) * (1.0 / n) + EPS)


def _gelu_tanh(x):
    c = math.sqrt(2.0 / math.pi)
    hx = 0.5 * x
    return hx + hx * jnp.tanh(x * (c + (c * 0.044715) * (x * x)))


def _sigmoid(x):
    return 1.0 / (1.0 + jnp.exp(-x))


def _mod_kernel(c_ref, w_ref, b_ref, o_ref):
    c = c_ref[...]
    a = (c * _sigmoid(c)).astype(BF16)
    o_ref[...] = _dot(a, w_ref[...].astype(BF16)) + b_ref[...]


def _modulation(c16, w_ada, b_ada):
    n_cols = N_MOD * D_MODEL
    return pl.pallas_call(
        _mod_kernel,
        out_shape=jax.ShapeDtypeStruct((DEPTH, MOD_ROWS, n_cols), F32),
        grid=(DEPTH, n_cols // MOD_TN),
        in_specs=[
            pl.BlockSpec((MOD_ROWS, D_MODEL), lambda l, j: (0, 0)),
            pl.BlockSpec((None, D_MODEL, MOD_TN), lambda l, j: (l, 0, j)),
            pl.BlockSpec((None, 1, MOD_TN), lambda l, j: (l, 0, j)),
        ],
        out_specs=pl.BlockSpec((None, MOD_ROWS, MOD_TN), lambda l, j: (l, 0, j)),
        compiler_params=pltpu.CompilerParams(
            dimension_semantics=("arbitrary", "arbitrary"),
            vmem_limit_bytes=VMEM_LIMIT),
        name="adaln_modulation",
    )(c16, w_ada, b_ada.reshape(DEPTH, 1, n_cols))


def _rope(x, c, s1, s2):
    return x * c + pltpu.roll(x, 16, axis=1) * s1 + pltpu.roll(x, LANES - 16, axis=1) * s2


def _mixer_in_kernel(x_ref, sh_ref, sc_ref, n1g_ref, wlat_ref, wbig_ref, gqa_ref, gkva_ref,
                     wqb_ref, wknb_ref, wvt_ref, gq_ref, gk_ref, vone_ref, tc_ref, ts1_ref,
                     ts2_ref, gsgu_ref, ws_ref, bs_ref, wob_ref,
                     q_ref, k_ref, vt_ref, sga_ref, pb_ref, prod_scr):
    tm = x_ref.shape[0]
    x = x_ref[...]
    h = (x * _rms_scale(x, D_MODEL)) * (n1g_ref[...] * (1.0 + sc_ref[...])) + sh_ref[...]
    hb = h.astype(BF16)

    lat = _dot(hb, wlat_ref[...])
    q_lat = lat[:, :Q_LORA]
    kv_lat = lat[:, Q_LORA:Q_LORA + KV_LORA]
    kr_slot = pltpu.roll(lat[:, Q_LORA + KV_LORA:], QK_NOPE, axis=1)
    qn = (q_lat * _rms_scale(q_lat, Q_LORA) * gqa_ref[...]).astype(BF16)
    kvn = (kv_lat * _rms_scale(kv_lat, KV_LORA) * gkva_ref[...]).astype(BF16)
    q_all = _dot(qn, wqb_ref[...])
    kn_all = _dot(kvn, wknb_ref[...])
    vt = lax.dot_general(wvt_ref[...], kvn, (((1,), (1,)), ((), ())), preferred_element_type=F32)
    vt_ref[...] = (vt + vone_ref[...]).astype(BF16)

    tc, ts1, ts2 = tc_ref[...], ts1_ref[...], ts2_ref[...]
    gq, gk = gq_ref[...], gk_ref[...]
    for hd in range(N_HEADS):
        sl = slice(hd * LANES, (hd + 1) * LANES)
        qh = q_all[:, sl]
        qh = qh * _rms_scale(qh, QK_DIM) * gq
        q_ref[:, sl] = _rope(qh, tc, ts1, ts2).astype(BF16)
        kh = kn_all[:, sl] + kr_slot
        kh = kh * _rms_scale(kh, QK_DIM) * gk
        k_ref[:, sl] = _rope(kh, tc, ts1, ts2).astype(BF16)

    u = _gelu_tanh(_dot(hb, wbig_ref[:, 0:D_MODEL]))
    gv = _gelu_tanh(_dot(hb, wbig_ref[:, D_MODEL:2 * D_MODEL]))
    vn = (gv * _rms_scale(gv, D_MODEL) * gsgu_ref[...]).astype(BF16)
    for c in range(tm // CHUNK):
        rows = slice(c * CHUNK, (c + 1) * CHUNK)
        for g in range(SGU_GROUPS):
            cols = slice(g * CHUNK, (g + 1) * CHUNK)
            mixed = _dot(ws_ref[g], vn[rows, cols]) + bs_ref[:, cols]
            prod_scr[rows, cols] = (u[rows, cols] * mixed).astype(BF16)
    out_b = _dot(prod_scr[...], wob_ref[...])

    sga_ref[...] = _sigmoid(_dot(hb, wbig_ref[:, 2 * D_MODEL:3 * D_MODEL]))
    pb_ref[...] = _sigmoid(_dot(hb, wbig_ref[:, 3 * D_MODEL:4 * D_MODEL])) * out_b


def _const_spec(shape, layer=None):
    if layer is None:
        return pl.BlockSpec(shape, lambda i: (0,) * len(shape), pipeline_mode=pl.Buffered(1))
    return pl.BlockSpec((None,) + shape, lambda i: (layer,) + (0,) * len(shape),
                        pipeline_mode=pl.Buffered(1))


def _mod_spec(layer, which, row0, tiles_per_seq):
    return pl.BlockSpec((None, None, None, 1, D_MODEL),
                        lambda i: (layer, row0 + i // tiles_per_seq, which, 0, 0))


def _mixer_in(layer, x2d, mod5, row0, seq, w, tabs):
    n_tok = x2d.shape[0]
    tm = TM_IN
    tps = seq // tm
    tile = lambda cols: pl.BlockSpec((tm, cols), lambda i: (i, 0))
    tab_spec = pl.BlockSpec((tm, LANES), lambda i: (i % tps, 0))
    in_specs = [
        tile(D_MODEL),
        _mod_spec(layer, 0, row0, tps),
        _mod_spec(layer, 1, row0, tps),
        _const_spec((1, D_MODEL), layer),
        _const_spec((D_MODEL, LAT_COLS), layer),
        _const_spec((D_MODEL, 4 * D_MODEL), layer),
        _const_spec((1, Q_LORA), layer),
        _const_spec((1, KV_LORA), layer),
        _const_spec((Q_LORA, SLOTS), layer),
        _const_spec((KV_LORA, SLOTS), layer),
        _const_spec((SLOTS, KV_LORA), layer),
        _const_spec((1, LANES), layer),
        _const_spec((1, LANES), layer),
        _const_spec((SLOTS, tm)),
        tab_spec, tab_spec, tab_spec,
        _const_spec((1, D_MODEL), layer),
        _const_spec((SGU_GROUPS, CHUNK, CHUNK), layer),
        _const_spec((CHUNK, D_MODEL), layer),
        _const_spec((D_MODEL, D_MODEL), layer),
    ]
    out_shape = (
        jax.ShapeDtypeStruct((n_tok, SLOTS), BF16),
        jax.ShapeDtypeStruct((n_tok, SLOTS), BF16),
        jax.ShapeDtypeStruct((n_tok // tm, SLOTS, tm), BF16),
        jax.ShapeDtypeStruct((n_tok, D_MODEL), F32),
        jax.ShapeDtypeStruct((n_tok, D_MODEL), F32),
    )
    out_specs = (tile(SLOTS), tile(SLOTS),
                 pl.BlockSpec((None, SLOTS, tm), lambda i: (i, 0, 0)),
                 tile(D_MODEL), tile(D_MODEL))
    return pl.pallas_call(
        _mixer_in_kernel,
        out_shape=out_shape,
        grid=(n_tok // tm,),
        in_specs=in_specs,
        out_specs=out_specs,
        scratch_shapes=[pltpu.VMEM((tm, D_MODEL), BF16)],
        compiler_params=pltpu.CompilerParams(
            dimension_semantics=("arbitrary",), vmem_limit_bytes=VMEM_LIMIT),
        name="mixer_in",
    )(x2d, mod5, mod5, w["norm1_g"], w["w_lat"], w["w_big"], w["q_a_norm_g"], w["kv_a_norm_g"],
      w["w_q_b"], w["w_kn_b"], w["w_v_t"], w["gq_slot"], w["gk_slot"], w["v_ones"],
      tabs[0], tabs[1], tabs[2], w["sgu_norm_g"], w["w_s"], w["b_s_full"], w["w_o_b"])


def _attn_kernel(q_ref, k_ref, vt_ref, o_ref, m_scr, acc_scr, st_scr, cm_scr, pt_scr, al_scr):
    tq = q_ref.shape[0]
    n_chunks = vt_ref.shape[0]
    m_scr[...] = jnp.full(m_scr.shape, -jnp.inf, F32)
    acc_scr[...] = jnp.zeros(acc_scr.shape, F32)

    def scores(hh, j, par):
        sl = slice(hh * LANES, (hh + 1) * LANES)
        kc = k_ref[pl.ds(pl.multiple_of(j * TK, TK), TK), sl]
        st = lax.dot_general(kc, q_ref[:, sl], (((1,), (1,)), ((), ())),
                             preferred_element_type=F32)
        st_scr[par, hh] = st
        cm_scr[par, hh] = jnp.max(st, axis=0, keepdims=True)

    def probs(hh, par):
        m_old = m_scr[hh]
        m_new = jnp.maximum(m_old, cm_scr[par, hh])
        m_scr[hh] = m_new
        al_scr[par, hh] = jnp.exp2(m_old - m_new)
        pt_scr[par, hh] = jnp.exp2(st_scr[par, hh] - m_new).astype(BF16)

    def accumulate(hh, j, par):
        sl = slice(hh * LANES, (hh + 1) * LANES)
        acc_scr[hh] = (al_scr[par, hh] * acc_scr[hh]
                       + _dot(vt_ref[j, sl, :], pt_scr[par, hh]))

    def step(j, par, do_scores=True, do_probs=True):
        for hh in range(2):
            if do_scores:
                scores(hh, j + 2, par)
            if do_probs:
                probs(hh, 1 - par)
            accumulate(hh, j, par)

    for hh in range(2):
        scores(hh, 0, 0)
        probs(hh, 0)
        scores(hh, 1, 1)

    def body(i, carry):
        step(2 * i, 0)
        step(2 * i + 1, 1)
        return carry

    lax.fori_loop(0, (n_chunks - 2) // 2, body, 0)
    step(n_chunks - 2, 0, do_scores=False)
    step(n_chunks - 1, 1, do_scores=False, do_probs=False)
    a0, a1 = acc_scr[0], acc_scr[1]
    o0 = a0 * (1.0 / a0[V_DIM:V_DIM + 1, :])
    o1 = a1 * (1.0 / a1[0:1, :])
    row = lax.broadcasted_iota(jnp.int32, (LANES, tq), 0)
    o_ref[...] = jnp.where(row < V_DIM, o0, o1).T.astype(BF16)


def _attention(q, k, vt, batch, seq):
    n_tok = q.shape[0]
    qt = seq // TQ
    pair = 2 * LANES
    return pl.pallas_call(
        _attn_kernel,
        out_shape=jax.ShapeDtypeStruct((n_tok, N_HEADS * V_DIM), BF16),
        grid=(batch, N_HEADS // 2, qt),
        in_specs=[
            pl.BlockSpec((TQ, pair), lambda b, hp, i: (b * qt + i, hp)),
            pl.BlockSpec((seq, pair), lambda b, hp, i: (b, hp)),
            pl.BlockSpec((seq // TK, pair, TK), lambda b, hp, i: (b, hp, 0)),
        ],
        out_specs=pl.BlockSpec((TQ, LANES), lambda b, hp, i: (b * qt + i, hp)),
        scratch_shapes=[
            pltpu.VMEM((2, 1, TQ), F32),
            pltpu.VMEM((2, LANES, TQ), F32),
            pltpu.VMEM((2, 2, TK, TQ), F32),
            pltpu.VMEM((2, 2, 1, TQ), F32),
            pltpu.VMEM((2, 2, TK, TQ), BF16),
            pltpu.VMEM((2, 2, 1, TQ), F32),
        ],
        compiler_params=pltpu.CompilerParams(
            dimension_semantics=("arbitrary", "arbitrary", "arbitrary"),
            vmem_limit_bytes=VMEM_LIMIT),
        name="attention",
    )(q, k, vt)


def _mixer_out_kernel(x_ref, o_ref, sga_ref, pb_ref, g1_ref, sh2_ref, sc2_ref, g2_ref,
                      woa_ref, wout_ref, n2g_ref, wfi_ref, wfo_ref, y_ref):
    out_a = _dot(o_ref[...], woa_ref[...])
    merged = (sga_ref[...] * out_a + pb_ref[...]).astype(BF16)
    x1 = x_ref[...] + g1_ref[...] * _dot(merged, wout_ref[...])
    h2 = (x1 * _rms_scale(x1, D_MODEL)) * (n2g_ref[...] * (1.0 + sc2_ref[...])) + sh2_ref[...]
    h2 = h2.astype(BF16)
    up = _dot(h2, wfi_ref[:, :D_FF])
    gate = _dot(h2, wfi_ref[:, D_FF:])
    act = (gate * _sigmoid(gate) * up).astype(BF16)
    y_ref[...] = x1 + g2_ref[...] * _dot(act, wfo_ref[...])


def _mixer_out(layer, x2d, o, sga, pb, mod5, row0, seq, w):
    n_tok = x2d.shape[0]
    tm = TM_OUT
    tps = seq // tm
    tile = lambda cols: pl.BlockSpec((tm, cols), lambda i: (i, 0))
    in_specs = [
        tile(D_MODEL), tile(N_HEADS * V_DIM), tile(D_MODEL), tile(D_MODEL),
        _mod_spec(layer, 2, row0, tps),
        _mod_spec(layer, 3, row0, tps),
        _mod_spec(layer, 4, row0, tps),
        _mod_spec(layer, 5, row0, tps),
        _const_spec((N_HEADS * V_DIM, D_MODEL), layer),
        _const_spec((D_MODEL, D_MODEL), layer),
        _const_spec((1, D_MODEL), layer),
        _const_spec((D_MODEL, 2 * D_FF), layer),
        _const_spec((D_FF, D_MODEL), layer),
    ]
    return pl.pallas_call(
        _mixer_out_kernel,
        out_shape=jax.ShapeDtypeStruct((n_tok, D_MODEL), F32),
        grid=(n_tok // tm,),
        in_specs=in_specs,
        out_specs=tile(D_MODEL),
        compiler_params=pltpu.CompilerParams(
            dimension_semantics=("arbitrary",), vmem_limit_bytes=VMEM_LIMIT),
        name="mixer_out",
    )(x2d, o, sga, pb, mod5, mod5, mod5, mod5,
      w["w_o_a"], w["w_out"], w["norm2_g"], w["w_ffn_in"], w["w_ffn_out"])


def _head_slots(w, width):
    L, K, _ = w.shape
    w = w.reshape(L, K, N_HEADS, width)
    w = jnp.pad(w, ((0, 0), (0, 0), (0, 0), (0, LANES - width)))
    return w.reshape(L, K, SLOTS)


def _rope_tables(seq):
    pos = jnp.arange(seq, dtype=F32)
    inv = ROPE_BASE ** (-jnp.arange(0, QK_ROPE, 2, dtype=F32) / QK_ROPE)
    ang = pos[:, None] * inv[None, :]
    cos, sin = jnp.cos(ang), jnp.sin(ang)
    half = QK_ROPE // 2
    ones = jnp.ones((seq, QK_NOPE), F32)
    zpad = jnp.zeros((seq, LANES - QK_DIM), F32)
    zh = jnp.zeros((seq, half), F32)
    zn = jnp.zeros((seq, QK_NOPE), F32)
    tc = jnp.concatenate([ones, cos, cos, zpad], axis=1)
    ts1 = jnp.concatenate([zn, zh, sin, zpad], axis=1)
    ts2 = jnp.concatenate([zn, -sin, zh, zpad], axis=1)
    return tc, ts1, ts2


def _prepare_weights(norm1_g, w_in, q_a_norm_g, kv_a_norm_g, w_q_b, w_kv_b, q_norm_g, k_norm_g,
                     w_o_a, sgu_norm_g, w_s, b_s, w_o_b, w_out, norm2_g, w_ffn_in, w_ffn_out):
    L = DEPTH
    lat_end = Q_LORA + KV_LORA + QK_ROPE
    w_lat = jnp.pad(w_in[:, :, :lat_end], ((0, 0), (0, 0), (0, LAT_COLS - lat_end))).astype(BF16)
    w_big = w_in[:, :, lat_end:].astype(BF16)
    kv4 = w_kv_b.reshape(L, KV_LORA, N_HEADS, QK_NOPE + V_DIM)
    w_kn = _head_slots(kv4[..., :QK_NOPE].reshape(L, KV_LORA, N_HEADS * QK_NOPE), QK_NOPE)
    wv = kv4[..., QK_NOPE:]
    zeros = jnp.zeros_like(wv)
    even = jnp.concatenate([wv, zeros], axis=-1)
    odd = jnp.concatenate([zeros, wv], axis=-1)
    is_even = (jnp.arange(N_HEADS) % 2 == 0)[None, None, :, None]
    w_v = jnp.where(is_even, even, odd).reshape(L, KV_LORA, SLOTS)
    v_ones = np.zeros((SLOTS, TM_IN), np.float32)
    for hd in range(N_HEADS):
        v_ones[hd * LANES + (V_DIM if hd % 2 == 0 else 0), :] = 1.0
    q_scale = (QK_DIM ** -0.5) * math.log2(math.e)
    pad_gain = lambda g: jnp.pad(g, ((0, 0), (0, LANES - QK_DIM))).reshape(L, 1, LANES)
    return {
        "norm1_g": norm1_g.reshape(L, 1, D_MODEL),
        "w_lat": w_lat,
        "w_big": w_big,
        "q_a_norm_g": q_a_norm_g.reshape(L, 1, Q_LORA),
        "kv_a_norm_g": kv_a_norm_g.reshape(L, 1, KV_LORA),
        "w_q_b": _head_slots(w_q_b, QK_DIM).astype(BF16),
        "w_kn_b": w_kn.astype(BF16),
        "w_v_t": jnp.swapaxes(w_v, 1, 2).astype(BF16),
        "gq_slot": pad_gain(q_norm_g * q_scale),
        "gk_slot": pad_gain(k_norm_g),
        "v_ones": jnp.asarray(v_ones),
        "sgu_norm_g": sgu_norm_g.reshape(L, 1, D_MODEL),
        "w_s": w_s.astype(BF16),
        "b_s_full": jnp.repeat(jnp.swapaxes(b_s, 1, 2), CHUNK, axis=2),
        "w_o_b": w_o_b.astype(BF16),
        "w_o_a": w_o_a.astype(BF16),
        "w_out": w_out.astype(BF16),
        "norm2_g": norm2_g.reshape(L, 1, D_MODEL),
        "w_ffn_in": w_ffn_in.astype(BF16),
        "w_ffn_out": w_ffn_out.astype(BF16),
    }


def kernel(x_prompt, x_sample, c_prompt, c_sample, w_ada, b_ada, norm1_g, w_in, q_a_norm_g,
           kv_a_norm_g, w_q_b, w_kv_b, q_norm_g, k_norm_g, w_o_a, sgu_norm_g, w_s, b_s, w_o_b,
           w_out, norm2_g, w_ffn_in, w_ffn_out):
    w = _prepare_weights(norm1_g, w_in, q_a_norm_g, kv_a_norm_g, w_q_b, w_kv_b, q_norm_g,
                         k_norm_g, w_o_a, sgu_norm_g, w_s, b_s, w_o_b, w_out, norm2_g,
                         w_ffn_in, w_ffn_out)
    n_prompt, n_sample = c_prompt.shape[0], c_sample.shape[0]
    c16 = jnp.concatenate(
        [c_prompt, c_sample, jnp.zeros((MOD_ROWS - n_prompt - n_sample, D_MODEL), F32)], axis=0)
    mod = _modulation(c16, w_ada, b_ada)
    mod5 = mod.reshape(DEPTH, MOD_ROWS, N_MOD, 1, D_MODEL)

    groups = []
    for x, row0 in ((x_prompt, 0), (x_sample, n_prompt)):
        batch, seq, _ = x.shape
        groups.append([x.reshape(batch * seq, D_MODEL), row0, batch, seq, _rope_tables(seq)])

    for layer in range(DEPTH):
        for grp in groups:
            x2d, row0, batch, seq, tabs = grp
            q, k, v, sga, pb = _mixer_in(layer, x2d, mod5, row0, seq, w, tabs)
            o = _attention(q, k, v, batch, seq)
            grp[0] = _mixer_out(layer, x2d, o, sga, pb, mod5, row0, seq, w)

    return tuple(grp[0].reshape(grp[2], grp[3], D_MODEL) for grp in groups)
```

```python
import functools
import math

import jax
import jax.numpy as jnp
import numpy as np
from jax import lax
from jax.experimental import pallas as pl
from jax.experimental.pallas import tpu as pltpu

D_MODEL = 1024
DEPTH = 4
N_HEADS = 8
QK_NOPE = 64
QK_ROPE = 32
QK_DIM = QK_NOPE + QK_ROPE
V_DIM = 64
Q_LORA = 384
KV_LORA = 256
ROPE_BASE = 10000.0
CHUNK = 128
SGU_GROUPS = 8
D_FF = 2816
N_MOD = 6
EPS = 1e-6

LANES = 128
SLOTS = N_HEADS * LANES
LAT_COLS = 768
MOD_ROWS = 16
VMEM_LIMIT = 56 * 1024 * 1024

TM_IN = 256
TM_OUT = 256
TQ = 1024
TK = TM_IN
TS = 256
MOD_TN = 512

F32 = jnp.float32
BF16 = jnp.bfloat16


def _dot(a, b):
    return jnp.dot(a, b, preferred_element_type=F32)


def _rms_scale(x, n):
    return lax.rsqrt(jnp.sum(x * x, axis=-1, keepdims=True) * (1.0 / n) + EPS)


def _gelu_tanh(x):
    c = math.sqrt(2.0 / math.pi)
    hx = 0.5 * x
    return hx + hx * jnp.tanh(x * (c + (c * 0.044715) * (x * x)))


def _sigmoid(x):
    return 1.0 / (1.0 + jnp.exp(-x))


def _mod_kernel(c_ref, w_ref, b_ref, o_ref):
    c = c_ref[...]
    a = (c * _sigmoid(c)).astype(BF16)
    o_ref[...] = _dot(a, w_ref[...].astype(BF16)) + b_ref[...]


def _modulation(c16, w_ada, b_ada):
    n_cols = N_MOD * D_MODEL
    return pl.pallas_call(
        _mod_kernel,
        out_shape=jax.ShapeDtypeStruct((DEPTH, MOD_ROWS, n_cols), F32),
        grid=(DEPTH, n_cols // MOD_TN),
        in_specs=[
            pl.BlockSpec((MOD_ROWS, D_MODEL), lambda l, j: (0, 0)),
            pl.BlockSpec((None, D_MODEL, MOD_TN), lambda l, j: (l, 0, j)),
            pl.BlockSpec((None, 1, MOD_TN), lambda l, j: (l, 0, j)),
        ],
        out_specs=pl.BlockSpec((None, MOD_ROWS, MOD_TN), lambda l, j: (l, 0, j)),
        compiler_params=pltpu.CompilerParams(
            dimension_semantics=("arbitrary", "arbitrary"),
            vmem_limit_bytes=VMEM_LIMIT),
        name="adaln_modulation",
    )(c16, w_ada, b_ada.reshape(DEPTH, 1, n_cols))


def _norm_rope(x, a, b):
    return _rms_scale(x, QK_DIM) * (x * a + pltpu.roll(x, LANES // 2, axis=1) * b)


def _mixer_in_kernel(x_ref, sh_ref, sc_ref, n1g_ref, wlat_ref, wbig_ref, gqa_ref, gkva_ref,
                     wqb_ref, wknb_ref, wvt_ref, gq_ref, gqr_ref, gk_ref, gkr_ref, vone_ref,
                     tc_ref, ts_ref, gsgu_ref, ws_ref, bs_ref, wob_ref,
                     q_ref, k_ref, vt_ref, sga_ref, pb_ref, prod_scr):
    tm = x_ref.shape[0]
    x = x_ref[...]
    h = (x * _rms_scale(x, D_MODEL)) * (n1g_ref[...] * (1.0 + sc_ref[...])) + sh_ref[...]
    hb = h.astype(BF16)

    lat = _dot(hb, wlat_ref[...])
    q_lat = lat[:, :Q_LORA]
    kv_lat = lat[:, Q_LORA:Q_LORA + KV_LORA]
    kr_slot = lat[:, Q_LORA + KV_LORA:]
    qn = (q_lat * _rms_scale(q_lat, Q_LORA) * gqa_ref[...]).astype(BF16)
    kvn = (kv_lat * _rms_scale(kv_lat, KV_LORA) * gkva_ref[...]).astype(BF16)
    q_all = _dot(qn, wqb_ref[...])
    kn_all = _dot(kvn, wknb_ref[...])
    vt = lax.dot_general(wvt_ref[...], kvn, (((1,), (1,)), ((), ())), preferred_element_type=F32)
    vt_ref[...] = (vt + vone_ref[...]).astype(BF16)

    tc, ts = tc_ref[...], ts_ref[...]
    qa, qb = tc * gq_ref[...], ts * gqr_ref[...]
    ka, kb = tc * gk_ref[...], ts * gkr_ref[...]

    def qk_heads(heads):
        for hd in heads:
            sl = slice(hd * LANES, (hd + 1) * LANES)
            q_ref[:, sl] = _norm_rope(q_all[:, sl], qa, qb).astype(BF16)
            k_ref[:, sl] = _norm_rope(kn_all[:, sl] + kr_slot, ka, kb).astype(BF16)

    zu = _dot(hb, wbig_ref[:, 0:D_MODEL])
    qk_heads(range(0, 2))
    zv = _dot(hb, wbig_ref[:, D_MODEL:2 * D_MODEL])
    qk_heads(range(2, 4))
    u = _gelu_tanh(zu)
    za = _dot(hb, wbig_ref[:, 2 * D_MODEL:3 * D_MODEL])
    qk_heads(range(4, 6))
    gv = _gelu_tanh(zv)
    vn = (gv * _rms_scale(gv, D_MODEL) * gsgu_ref[...]).astype(BF16)
    zb = _dot(hb, wbig_ref[:, 3 * D_MODEL:4 * D_MODEL])
    qk_heads(range(6, 8))

    for c in range(0, tm // CHUNK, 2):
        rows0 = slice(c * CHUNK, (c + 1) * CHUNK)
        rows1 = slice((c + 1) * CHUNK, (c + 2) * CHUNK)
        for g in range(SGU_GROUPS):
            cols = slice(g * CHUNK, (g + 1) * CHUNK)
            pair = jnp.concatenate([vn[rows0, cols], vn[rows1, cols]], axis=1)
            mixed = _dot(ws_ref[g], pair)
            bias = bs_ref[:, cols]
            prod_scr[rows0, cols] = (u[rows0, cols] * (mixed[:, :CHUNK] + bias)).astype(BF16)
            prod_scr[rows1, cols] = (u[rows1, cols] * (mixed[:, CHUNK:] + bias)).astype(BF16)
    sga_ref[...] = _sigmoid(za)
    out_b = _dot(prod_scr[...], wob_ref[...])
    pb_ref[...] = _sigmoid(zb) * out_b


def _const_spec(shape, layer=None):
    if layer is None:
        return pl.BlockSpec(shape, lambda i: (0,) * len(shape), pipeline_mode=pl.Buffered(1))
    return pl.BlockSpec((None,) + shape, lambda i: (layer,) + (0,) * len(shape),
                        pipeline_mode=pl.Buffered(1))


def _mod_spec(layer, which, row0, tiles_per_seq):
    return pl.BlockSpec((None, None, None, 1, D_MODEL),
                        lambda i: (layer, row0 + i // tiles_per_seq, which, 0, 0))


def _mixer_in(layer, x2d, mod5, row0, seq, w, tabs):
    n_tok = x2d.shape[0]
    tm = TM_IN
    tps = seq // tm
    tile = lambda cols: pl.BlockSpec((tm, cols), lambda i: (i, 0))
    tab_spec = pl.BlockSpec((tm, LANES), lambda i: (i % tps, 0))
    in_specs = [
        tile(D_MODEL),
        _mod_spec(layer, 0, row0, tps),
        _mod_spec(layer, 1, row0, tps),
        _const_spec((1, D_MODEL), layer),
        _const_spec((D_MODEL, LAT_COLS), layer),
        _const_spec((D_MODEL, 4 * D_MODEL), layer),
        _const_spec((1, Q_LORA), layer),
        _const_spec((1, KV_LORA), layer),
        _const_spec((Q_LORA, SLOTS), layer),
        _const_spec((KV_LORA, SLOTS), layer),
        _const_spec((SLOTS, KV_LORA), layer),
        _const_spec((1, LANES), layer),
        _const_spec((1, LANES), layer),
        _const_spec((1, LANES), layer),
        _const_spec((1, LANES), layer),
        _const_spec((SLOTS, tm)),
        tab_spec, tab_spec,
        _const_spec((1, D_MODEL), layer),
        _const_spec((SGU_GROUPS, CHUNK, CHUNK), layer),
        _const_spec((CHUNK, D_MODEL), layer),
        _const_spec((D_MODEL, D_MODEL), layer),
    ]
    out_shape = (
        jax.ShapeDtypeStruct((n_tok, SLOTS), BF16),
        jax.ShapeDtypeStruct((n_tok, SLOTS), BF16),
        jax.ShapeDtypeStruct((n_tok // tm, SLOTS, tm), BF16),
        jax.ShapeDtypeStruct((n_tok, D_MODEL), F32),
        jax.ShapeDtypeStruct((n_tok, D_MODEL), F32),
    )
    out_specs = (tile(SLOTS), tile(SLOTS),
                 pl.BlockSpec((None, SLOTS, tm), lambda i: (i, 0, 0)),
                 tile(D_MODEL), tile(D_MODEL))
    return pl.pallas_call(
        _mixer_in_kernel,
        out_shape=out_shape,
        grid=(n_tok // tm,),
        in_specs=in_specs,
        out_specs=out_specs,
        scratch_shapes=[pltpu.VMEM((tm, D_MODEL), BF16)],
        compiler_params=pltpu.CompilerParams(
            dimension_semantics=("arbitrary",), vmem_limit_bytes=VMEM_LIMIT),
        name="mixer_in",
    )(x2d, mod5, mod5, w["norm1_g"], w["w_lat"], w["w_big"], w["q_a_norm_g"], w["kv_a_norm_g"],
      w["w_q_b"], w["w_kn_b"], w["w_v_t"], w["gq_slot"], w["gq_rolled"], w["gk_slot"],
      w["gk_rolled"], w["v_ones"], tabs[0], tabs[1], w["sgu_norm_g"], w["w_s"], w["b_s_full"],
      w["w_o_b"])


def _attn_kernel(q_ref, k_ref, vt_ref, o_ref, m_scr, acc_scr, *parity_bufs):
    tq = q_ref.shape[0]
    n_chunks = vt_ref.shape[0]
    m_scr[...] = jnp.full(m_scr.shape, -jnp.inf, F32)
    acc_scr[...] = jnp.zeros(acc_scr.shape, F32)
    st_scr, cm_scr, pt_scr, al_scr = (parity_bufs[k::4] for k in range(4))

    items = [(hh, sub) for hh in range(2) for sub in range(tq // TS)]

    def scores(item, j, par):
        hh, sub = item
        sl = slice(hh * LANES, (hh + 1) * LANES)
        cs = slice(sub * TS, (sub + 1) * TS)
        kc = k_ref[pl.ds(pl.multiple_of(j * TK, TK), TK), sl]
        st = lax.dot_general(kc, q_ref[cs, sl], (((1,), (1,)), ((), ())),
                             preferred_element_type=F32)
        st_scr[par][hh, :, cs] = st
        cm_scr[par][hh, :, cs] = jnp.max(st, axis=0, keepdims=True)

    def probs(item, par):
        hh, sub = item
        cs = slice(sub * TS, (sub + 1) * TS)
        m_old = m_scr[hh, :, cs]
        m_new = jnp.maximum(m_old, cm_scr[par][hh, :, cs])
        m_scr[hh, :, cs] = m_new
        al_scr[par][hh, :, cs] = jnp.exp2(m_old - m_new)
        pt_scr[par][hh, :, cs] = jnp.exp2(st_scr[par][hh, :, cs] - m_new).astype(BF16)

    def accumulate(item, j, par):
        hh, sub = item
        sl = slice(hh * LANES, (hh + 1) * LANES)
        cs = slice(sub * TS, (sub + 1) * TS)
        acc_scr[hh, :, cs] = (al_scr[par][hh, :, cs] * acc_scr[hh, :, cs]
                              + _dot(vt_ref[j, sl, :], pt_scr[par][hh, :, cs]))

    def step(j, par, do_scores=True, do_probs=True):
        for item in items:
            if do_scores:
                scores(item, j + 2, par)
            if do_probs:
                probs(item, 1 - par)
            accumulate(item, j, par)

    for item in items:
        scores(item, 0, 0)
        probs(item, 0)
        scores(item, 1, 1)

    def body(i, carry):
        step(2 * i, 0)
        step(2 * i + 1, 1)
        return carry

    lax.fori_loop(0, (n_chunks - 2) // 2, body, 0)
    step(n_chunks - 2, 0, do_scores=False)
    step(n_chunks - 1, 1, do_scores=False, do_probs=False)
    a0, a1 = acc_scr[0], acc_scr[1]
    o0 = a0 * (1.0 / a0[V_DIM:V_DIM + 1, :])
    o1 = a1 * (1.0 / a1[0:1, :])
    row = lax.broadcasted_iota(jnp.int32, (LANES, tq), 0)
    o_ref[...] = jnp.where(row < V_DIM, o0, o1).T.astype(BF16)


def _attention(q, k, vt, batch, seq):
    n_tok = q.shape[0]
    qt = seq // TQ
    pair = 2 * LANES
    return pl.pallas_call(
        _attn_kernel,
        out_shape=jax.ShapeDtypeStruct((n_tok, N_HEADS * V_DIM), BF16),
        grid=(batch, N_HEADS // 2, qt),
        in_specs=[
            pl.BlockSpec((TQ, pair), lambda b, hp, i: (b * qt + i, hp)),
            pl.BlockSpec((seq, pair), lambda b, hp, i: (b, hp)),
            pl.BlockSpec((seq // TK, pair, TK), lambda b, hp, i: (b, hp, 0)),
        ],
        out_specs=pl.BlockSpec((TQ, LANES), lambda b, hp, i: (b * qt + i, hp)),
        scratch_shapes=[
            pltpu.VMEM((2, 1, TQ), F32),
            pltpu.VMEM((2, LANES, TQ), F32),
        ] + 2 * [
            pltpu.VMEM((2, TK, TQ), F32),
            pltpu.VMEM((2, 1, TQ), F32),
            pltpu.VMEM((2, TK, TQ), BF16),
            pltpu.VMEM((2, 1, TQ), F32),
        ],
        compiler_params=pltpu.CompilerParams(
            dimension_semantics=("arbitrary", "arbitrary", "arbitrary"),
            vmem_limit_bytes=VMEM_LIMIT),
        name="attention",
    )(q, k, vt)


def _mixer_out_kernel(x_ref, o_ref, sga_ref, pb_ref, g1_ref, sh2_ref, sc2_ref, g2_ref,
                      woa_ref, wout_ref, n2g_ref, wfi_ref, wfo_ref, y_ref):
    out_a = _dot(o_ref[...], woa_ref[...])
    merged = (sga_ref[...] * out_a + pb_ref[...]).astype(BF16)
    x1 = x_ref[...] + g1_ref[...] * _dot(merged, wout_ref[...])
    h2 = (x1 * _rms_scale(x1, D_MODEL)) * (n2g_ref[...] * (1.0 + sc2_ref[...])) + sh2_ref[...]
    h2 = h2.astype(BF16)
    up = _dot(h2, wfi_ref[:, :D_FF])
    gate = _dot(h2, wfi_ref[:, D_FF:])
    act = (gate * _sigmoid(gate) * up).astype(BF16)
    y_ref[...] = x1 + g2_ref[...] * _dot(act, wfo_ref[...])


def _mixer_out(layer, x2d, o, sga, pb, mod5, row0, seq, w):
    n_tok = x2d.shape[0]
    tm = TM_OUT
    tps = seq // tm
    tile = lambda cols: pl.BlockSpec((tm, cols), lambda i: (i, 0))
    in_specs = [
        tile(D_MODEL), tile(N_HEADS * V_DIM), tile(D_MODEL), tile(D_MODEL),
        _mod_spec(layer, 2, row0, tps),
        _mod_spec(layer, 3, row0, tps),
        _mod_spec(layer, 4, row0, tps),
        _mod_spec(layer, 5, row0, tps),
        _const_spec((N_HEADS * V_DIM, D_MODEL), layer),
        _const_spec((D_MODEL, D_MODEL), layer),
        _const_spec((1, D_MODEL), layer),
        _const_spec((D_MODEL, 2 * D_FF), layer),
        _const_spec((D_FF, D_MODEL), layer),
    ]
    return pl.pallas_call(
        _mixer_out_kernel,
        out_shape=jax.ShapeDtypeStruct((n_tok, D_MODEL), F32),
        grid=(n_tok // tm,),
        in_specs=in_specs,
        out_specs=tile(D_MODEL),
        compiler_params=pltpu.CompilerParams(
            dimension_semantics=("arbitrary",), vmem_limit_bytes=VMEM_LIMIT),
        name="mixer_out",
    )(x2d, o, sga, pb, mod5, mod5, mod5, mod5,
      w["w_o_a"], w["w_out"], w["norm2_g"], w["w_ffn_in"], w["w_ffn_out"])


def _slot_source():
    half = QK_ROPE // 2
    src = np.full((LANES,), QK_DIM, np.int32)
    src[0:48] = np.arange(0, 48)
    src[48:48 + half] = np.arange(QK_NOPE, QK_NOPE + half)
    src[64:80] = np.arange(48, QK_NOPE)
    src[LANES - half:] = np.arange(QK_NOPE + half, QK_DIM)
    return src


def _slot_lanes(a):
    a = jnp.pad(a, [(0, 0)] * (a.ndim - 1) + [(0, 1)])
    return jnp.take(a, _slot_source(), axis=-1)


def _head_slots(w, first_dim, width):
    L, K, _ = w.shape
    w = w.reshape(L, K, N_HEADS, width)
    w = jnp.pad(w, ((0, 0), (0, 0), (0, 0), (first_dim, QK_DIM - first_dim - width)))
    return _slot_lanes(w).reshape(L, K, SLOTS)


def _rope_tables(seq):
    pos = jnp.arange(seq, dtype=F32)
    inv = ROPE_BASE ** (-jnp.arange(0, QK_ROPE, 2, dtype=F32) / QK_ROPE)
    ang = pos[:, None] * inv[None, :]
    cos, sin = jnp.cos(ang), jnp.sin(ang)
    ones = jnp.ones((seq, QK_NOPE), F32)
    zn = jnp.zeros((seq, QK_NOPE), F32)
    tc = _slot_lanes(jnp.concatenate([ones, cos, cos], axis=1))
    ts = _slot_lanes(jnp.concatenate([zn, -sin, sin], axis=1))
    return tc, ts


def _prepare_weights(norm1_g, w_in, q_a_norm_g, kv_a_norm_g, w_q_b, w_kv_b, q_norm_g, k_norm_g,
                     w_o_a, sgu_norm_g, w_s, b_s, w_o_b, w_out, norm2_g, w_ffn_in, w_ffn_out):
    L = DEPTH
    lat_end = Q_LORA + KV_LORA + QK_ROPE
    w_kr = jnp.pad(w_in[:, :, Q_LORA + KV_LORA:lat_end], ((0, 0), (0, 0), (QK_NOPE, 0)))
    w_lat = jnp.concatenate([w_in[:, :, :Q_LORA + KV_LORA], _slot_lanes(w_kr)],
                            axis=-1).astype(BF16)
    w_big = w_in[:, :, lat_end:].astype(BF16)
    kv4 = w_kv_b.reshape(L, KV_LORA, N_HEADS, QK_NOPE + V_DIM)
    w_kn = _head_slots(kv4[..., :QK_NOPE].reshape(L, KV_LORA, N_HEADS * QK_NOPE), 0, QK_NOPE)
    wv = kv4[..., QK_NOPE:]
    zeros = jnp.zeros_like(wv)
    even = jnp.concatenate([wv, zeros], axis=-1)
    odd = jnp.concatenate([zeros, wv], axis=-1)
    is_even = (jnp.arange(N_HEADS) % 2 == 0)[None, None, :, None]
    w_v = jnp.where(is_even, even, odd).reshape(L, KV_LORA, SLOTS)
    v_ones = np.zeros((SLOTS, TM_IN), np.float32)
    for hd in range(N_HEADS):
        v_ones[hd * LANES + (V_DIM if hd % 2 == 0 else 0), :] = 1.0
    q_scale = (QK_DIM ** -0.5) * math.log2(math.e)
    gq_slot = _slot_lanes(q_norm_g * q_scale).reshape(L, 1, LANES)
    gk_slot = _slot_lanes(k_norm_g).reshape(L, 1, LANES)
    return {
        "norm1_g": norm1_g.reshape(L, 1, D_MODEL),
        "w_lat": w_lat,
        "w_big": w_big,
        "q_a_norm_g": q_a_norm_g.reshape(L, 1, Q_LORA),
        "kv_a_norm_g": kv_a_norm_g.reshape(L, 1, KV_LORA),
        "w_q_b": _head_slots(w_q_b, 0, QK_DIM).astype(BF16),
        "w_kn_b": w_kn.astype(BF16),
        "w_v_t": jnp.swapaxes(w_v, 1, 2).astype(BF16),
        "gq_slot": gq_slot,
        "gq_rolled": jnp.roll(gq_slot, LANES // 2, axis=-1),
        "gk_slot": gk_slot,
        "gk_rolled": jnp.roll(gk_slot, LANES // 2, axis=-1),
        "v_ones": jnp.asarray(v_ones),
        "sgu_norm_g": sgu_norm_g.reshape(L, 1, D_MODEL),
        "w_s": w_s.astype(BF16),
        "b_s_full": jnp.repeat(jnp.swapaxes(b_s, 1, 2), CHUNK, axis=2),
        "w_o_b": w_o_b.astype(BF16),
        "w_o_a": w_o_a.astype(BF16),
        "w_out": w_out.astype(BF16),
        "norm2_g": norm2_g.reshape(L, 1, D_MODEL),
        "w_ffn_in": w_ffn_in.astype(BF16),
        "w_ffn_out": w_ffn_out.astype(BF16),
    }


def kernel(x_prompt, x_sample, c_prompt, c_sample, w_ada, b_ada, norm1_g, w_in, q_a_norm_g,
           kv_a_norm_g, w_q_b, w_kv_b, q_norm_g, k_norm_g, w_o_a, sgu_norm_g, w_s, b_s, w_o_b,
           w_out, norm2_g, w_ffn_in, w_ffn_out):
    w = _prepare_weights(norm1_g, w_in, q_a_norm_g, kv_a_norm_g, w_q_b, w_kv_b, q_norm_g,
                         k_norm_g, w_o_a, sgu_norm_g, w_s, b_s, w_o_b, w_out, norm2_g,
                         w_ffn_in, w_ffn_out)
    n_prompt, n_sample = c_prompt.shape[0], c_sample.shape[0]
    c16 = jnp.concatenate(
        [c_prompt, c_sample, jnp.zeros((MOD_ROWS - n_prompt - n_sample, D_MODEL), F32)], axis=0)
    mod = _modulation(c16, w_ada, b_ada)
    mod5 = mod.reshape(DEPTH, MOD_ROWS, N_MOD, 1, D_MODEL)

    groups = []
    for x, row0 in ((x_prompt, 0), (x_sample, n_prompt)):
        batch, seq, _ = x.shape
        groups.append([x.reshape(batch * seq, D_MODEL), row0, batch, seq, _rope_tables(seq)])

    for layer in range(DEPTH):
        for grp in groups:
            x2d, row0, batch, seq, tabs = grp
            q, k, v, sga, pb = _mixer_in(layer, x2d, mod5, row0, seq, w, tabs)
            o = _attention(q, k, v, batch, seq)
            grp[0] = _mixer_out(layer, x2d, o, sga, pb, mod5, row0, seq, w)

    return tuple(grp[0].reshape(grp[2], grp[3], D_MODEL) for grp in groups)
```

```python
import functools
import math

import jax
import jax.numpy as jnp
import numpy as np
from jax import lax
from jax.experimental import pallas as pl
from jax.experimental.pallas import tpu as pltpu

D_MODEL = 1024
DEPTH = 4
N_HEADS = 8
QK_NOPE = 64
QK_ROPE = 32
QK_DIM = QK_NOPE + QK_ROPE
V_DIM = 64
Q_LORA = 384
KV_LORA = 256
ROPE_BASE = 10000.0
CHUNK = 128
SGU_GROUPS = 8
D_FF = 2816
N_MOD = 6
EPS = 1e-6

LANES = 128
SLOTS = N_HEADS * LANES
LAT_COLS = 768
MOD_ROWS = 16
VMEM_LIMIT = 56 * 1024 * 1024

TM_IN = 256
TM_OUT = 256
TQ = 1024
TK = TM_IN
TS = 256
STEPS_PER_BODY = 4
MOD_TN = 512

F32 = jnp.float32
BF16 = jnp.bfloat16


def _dot(a, b):
    return jnp.dot(a, b, preferred_element_type=F32)


def _rms_scale(x, n):
    return lax.rsqrt(jnp.sum(x * x, axis=-1, keepdims=True) * (1.0 / n) + EPS)


def _gelu_tanh(x):
    c = math.sqrt(2.0 / math.pi)
    hx = 0.5 * x
    return hx + hx * jnp.tanh(x * (c + (c * 0.044715) * (x * x)))


def _sigmoid(x):
    return 1.0 / (1.0 + jnp.exp(-x))


def _mod_kernel(c_ref, w_ref, b_ref, o_ref):
    c = c_ref[...]
    a = (c * _sigmoid(c)).astype(BF16)
    o_ref[...] = _dot(a, w_ref[...].astype(BF16)) + b_ref[...]


def _modulation(c16, w_ada, b_ada):
    n_cols = N_MOD * D_MODEL
    return pl.pallas_call(
        _mod_kernel,
        out_shape=jax.ShapeDtypeStruct((DEPTH, MOD_ROWS, n_cols), F32),
        grid=(DEPTH, n_cols // MOD_TN),
        in_specs=[
            pl.BlockSpec((MOD_ROWS, D_MODEL), lambda l, j: (0, 0)),
            pl.BlockSpec((None, D_MODEL, MOD_TN), lambda l, j: (l, 0, j)),
            pl.BlockSpec((None, 1, MOD_TN), lambda l, j: (l, 0, j)),
        ],
        out_specs=pl.BlockSpec((None, MOD_ROWS, MOD_TN), lambda l, j: (l, 0, j)),
        compiler_params=pltpu.CompilerParams(
            dimension_semantics=("arbitrary", "arbitrary"),
            vmem_limit_bytes=VMEM_LIMIT),
        name="adaln_modulation",
    )(c16, w_ada, b_ada.reshape(DEPTH, 1, n_cols))


def _norm_rope(x, a, b):
    return _rms_scale(x, QK_DIM) * (x * a + pltpu.roll(x, LANES // 2, axis=1) * b)


def _mixer_in_kernel(x_ref, sh_ref, sc_ref, n1g_ref, wlat_ref, wbig_ref, gqa_ref, gkva_ref,
                     wqb_ref, wknb_ref, wvt_ref, gq_ref, gqr_ref, gk_ref, gkr_ref, vone_ref,
                     tc_ref, ts_ref, gsgu_ref, ws_ref, bs_ref, wob_ref,
                     q_ref, k_ref, vt_ref, sga_ref, pb_ref, prod_scr):
    tm = x_ref.shape[0]
    x = x_ref[...]
    h = (x * _rms_scale(x, D_MODEL)) * (n1g_ref[...] * (1.0 + sc_ref[...])) + sh_ref[...]
    hb = h.astype(BF16)

    lat = _dot(hb, wlat_ref[...])
    q_lat = lat[:, :Q_LORA]
    kv_lat = lat[:, Q_LORA:Q_LORA + KV_LORA]
    kr_slot = lat[:, Q_LORA + KV_LORA:]
    qn = (q_lat * _rms_scale(q_lat, Q_LORA) * gqa_ref[...]).astype(BF16)
    kvn = (kv_lat * _rms_scale(kv_lat, KV_LORA) * gkva_ref[...]).astype(BF16)
    q_all = _dot(qn, wqb_ref[...])
    kn_all = _dot(kvn, wknb_ref[...])
    vt = lax.dot_general(wvt_ref[...], kvn, (((1,), (1,)), ((), ())), preferred_element_type=F32)
    vt_ref[...] = (vt + vone_ref[...]).astype(BF16)

    tc, ts = tc_ref[...], ts_ref[...]
    qa, qb = tc * gq_ref[...], ts * gqr_ref[...]
    ka, kb = tc * gk_ref[...], ts * gkr_ref[...]

    def qk_heads(heads):
        for hd in heads:
            sl = slice(hd * LANES, (hd + 1) * LANES)
            q_ref[:, sl] = _norm_rope(q_all[:, sl], qa, qb).astype(BF16)
            k_ref[:, sl] = _norm_rope(kn_all[:, sl] + kr_slot, ka, kb).astype(BF16)

    zu = _dot(hb, wbig_ref[:, 0:D_MODEL])
    qk_heads(range(0, 2))
    zv = _dot(hb, wbig_ref[:, D_MODEL:2 * D_MODEL])
    qk_heads(range(2, 4))
    u = _gelu_tanh(zu)
    za = _dot(hb, wbig_ref[:, 2 * D_MODEL:3 * D_MODEL])
    qk_heads(range(4, 6))
    gv = _gelu_tanh(zv)
    vn = (gv * _rms_scale(gv, D_MODEL) * gsgu_ref[...]).astype(BF16)
    zb = _dot(hb, wbig_ref[:, 3 * D_MODEL:4 * D_MODEL])
    qk_heads(range(6, 8))

    for c in range(0, tm // CHUNK, 2):
        rows0 = slice(c * CHUNK, (c + 1) * CHUNK)
        rows1 = slice((c + 1) * CHUNK, (c + 2) * CHUNK)
        for g in range(SGU_GROUPS):
            cols = slice(g * CHUNK, (g + 1) * CHUNK)
            pair = jnp.concatenate([vn[rows0, cols], vn[rows1, cols]], axis=1)
            mixed = _dot(ws_ref[g], pair)
            bias = bs_ref[:, cols]
            prod_scr[rows0, cols] = (u[rows0, cols] * (mixed[:, :CHUNK] + bias)).astype(BF16)
            prod_scr[rows1, cols] = (u[rows1, cols] * (mixed[:, CHUNK:] + bias)).astype(BF16)
    sga_ref[...] = _sigmoid(za)
    out_b = _dot(prod_scr[...], wob_ref[...])
    pb_ref[...] = _sigmoid(zb) * out_b


def _const_spec(shape, layer=None):
    if layer is None:
        return pl.BlockSpec(shape, lambda i: (0,) * len(shape), pipeline_mode=pl.Buffered(1))
    return pl.BlockSpec((None,) + shape, lambda i: (layer,) + (0,) * len(shape),
                        pipeline_mode=pl.Buffered(1))


def _mod_spec(layer, which, row0, tiles_per_seq):
    return pl.BlockSpec((None, None, None, 1, D_MODEL),
                        lambda i: (layer, row0 + i // tiles_per_seq, which, 0, 0))


def _mixer_in(layer, x2d, mod5, row0, seq, w, tabs):
    n_tok = x2d.shape[0]
    tm = TM_IN
    tps = seq // tm
    tile = lambda cols: pl.BlockSpec((tm, cols), lambda i: (i, 0))
    tab_spec = pl.BlockSpec((tm, LANES), lambda i: (i % tps, 0))
    in_specs = [
        tile(D_MODEL),
        _mod_spec(layer, 0, row0, tps),
        _mod_spec(layer, 1, row0, tps),
        _const_spec((1, D_MODEL), layer),
        _const_spec((D_MODEL, LAT_COLS), layer),
        _const_spec((D_MODEL, 4 * D_MODEL), layer),
        _const_spec((1, Q_LORA), layer),
        _const_spec((1, KV_LORA), layer),
        _const_spec((Q_LORA, SLOTS), layer),
        _const_spec((KV_LORA, SLOTS), layer),
        _const_spec((SLOTS, KV_LORA), layer),
        _const_spec((1, LANES), layer),
        _const_spec((1, LANES), layer),
        _const_spec((1, LANES), layer),
        _const_spec((1, LANES), layer),
        _const_spec((SLOTS, tm)),
        tab_spec, tab_spec,
        _const_spec((1, D_MODEL), layer),
        _const_spec((SGU_GROUPS, CHUNK, CHUNK), layer),
        _const_spec((CHUNK, D_MODEL), layer),
        _const_spec((D_MODEL, D_MODEL), layer),
    ]
    out_shape = (
        jax.ShapeDtypeStruct((n_tok, SLOTS), BF16),
        jax.ShapeDtypeStruct((n_tok, SLOTS), BF16),
        jax.ShapeDtypeStruct((n_tok // tm, SLOTS, tm), BF16),
        jax.ShapeDtypeStruct((n_tok, D_MODEL), F32),
        jax.ShapeDtypeStruct((n_tok, D_MODEL), F32),
    )
    out_specs = (tile(SLOTS), tile(SLOTS),
                 pl.BlockSpec((None, SLOTS, tm), lambda i: (i, 0, 0)),
                 tile(D_MODEL), tile(D_MODEL))
    return pl.pallas_call(
        _mixer_in_kernel,
        out_shape=out_shape,
        grid=(n_tok // tm,),
        in_specs=in_specs,
        out_specs=out_specs,
        scratch_shapes=[pltpu.VMEM((tm, D_MODEL), BF16)],
        compiler_params=pltpu.CompilerParams(
            dimension_semantics=("arbitrary",), vmem_limit_bytes=VMEM_LIMIT),
        name="mixer_in",
    )(x2d, mod5, mod5, w["norm1_g"], w["w_lat"], w["w_big"], w["q_a_norm_g"], w["kv_a_norm_g"],
      w["w_q_b"], w["w_kn_b"], w["w_v_t"], w["gq_slot"], w["gq_rolled"], w["gk_slot"],
      w["gk_rolled"], w["v_ones"], tabs[0], tabs[1], w["sgu_norm_g"], w["w_s"], w["b_s_full"],
      w["w_o_b"])


def _attn_kernel(q_ref, k_ref, vt_ref, o_ref, m_scr, acc_scr, *parity_bufs):
    tq = q_ref.shape[0]
    n_chunks = vt_ref.shape[0]
    m_scr[...] = jnp.full(m_scr.shape, -jnp.inf, F32)
    acc_scr[...] = jnp.zeros(acc_scr.shape, F32)
    st_scr, cm_scr = parity_bufs[0::2], parity_bufs[1::2]

    items = [(hh, sub) for hh in range(2) for sub in range(tq // TS)]

    def scores(item, j, par):
        hh, sub = item
        sl = slice(hh * LANES, (hh + 1) * LANES)
        cs = slice(sub * TS, (sub + 1) * TS)
        kc = k_ref[pl.ds(pl.multiple_of(j * TK, TK), TK), sl]
        st = lax.dot_general(kc, q_ref[cs, sl], (((1,), (1,)), ((), ())),
                             preferred_element_type=F32)
        st_scr[par][hh, :, cs] = st
        cm_scr[par][hh, :, cs] = jnp.max(st, axis=0, keepdims=True)

    def probs(item, par):
        hh, sub = item
        cs = slice(sub * TS, (sub + 1) * TS)
        m_old = m_scr[hh, :, cs]
        m_new = jnp.maximum(m_old, cm_scr[par][hh, :, cs])
        m_scr[hh, :, cs] = m_new
        pt = jnp.exp2(st_scr[par][hh, :, cs] - m_new).astype(BF16)
        return pt, jnp.exp2(m_old - m_new)

    def accumulate(item, j, pt, alpha):
        hh, sub = item
        sl = slice(hh * LANES, (hh + 1) * LANES)
        cs = slice(sub * TS, (sub + 1) * TS)
        acc_scr[hh, :, cs] = alpha * acc_scr[hh, :, cs] + _dot(vt_ref[j, sl, :], pt)

    def step(j, par, do_scores=True):
        for item in items:
            pt, alpha = probs(item, par)
            if do_scores:
                scores(item, j + 1, 1 - par)
            accumulate(item, j, pt, alpha)

    for item in items:
        scores(item, 0, 0)

    spb = STEPS_PER_BODY if n_chunks > 2 * STEPS_PER_BODY else 2

    def body(i, carry):
        for s in range(spb):
            step(spb * i + s, s % 2)
        return carry

    full = (n_chunks - 1) // spb
    lax.fori_loop(0, full, body, 0)
    for j in range(spb * full, n_chunks - 1):
        step(j, j % 2)
    step(n_chunks - 1, (n_chunks - 1) % 2, do_scores=False)
    a0, a1 = acc_scr[0], acc_scr[1]
    o0 = a0 * (1.0 / a0[V_DIM:V_DIM + 1, :])
    o1 = a1 * (1.0 / a1[0:1, :])
    row = lax.broadcasted_iota(jnp.int32, (LANES, tq), 0)
    o_ref[...] = jnp.where(row < V_DIM, o0, o1).T.astype(BF16)


def _attention(q, k, vt, batch, seq):
    n_tok = q.shape[0]
    tq = seq if seq <= 2 * TQ else TQ
    qt = seq // tq
    pair = 2 * LANES
    return pl.pallas_call(
        _attn_kernel,
        out_shape=jax.ShapeDtypeStruct((n_tok, N_HEADS * V_DIM), BF16),
        grid=(batch, N_HEADS // 2, qt),
        in_specs=[
            pl.BlockSpec((tq, pair), lambda b, hp, i: (b * qt + i, hp)),
            pl.BlockSpec((seq, pair), lambda b, hp, i: (b, hp)),
            pl.BlockSpec((seq // TK, pair, TK), lambda b, hp, i: (b, hp, 0)),
        ],
        out_specs=pl.BlockSpec((tq, LANES), lambda b, hp, i: (b * qt + i, hp)),
        scratch_shapes=[
            pltpu.VMEM((2, 1, tq), F32),
            pltpu.VMEM((2, LANES, tq), F32),
        ] + 2 * [
            pltpu.VMEM((2, TK, tq), F32),
            pltpu.VMEM((2, 1, tq), F32),
        ],
        compiler_params=pltpu.CompilerParams(
            dimension_semantics=("arbitrary", "arbitrary", "arbitrary"),
            vmem_limit_bytes=VMEM_LIMIT),
        name="attention",
    )(q, k, vt)


def _mixer_out_kernel(x_ref, o_ref, sga_ref, pb_ref, g1_ref, sh2_ref, sc2_ref, g2_ref,
                      woa_ref, wout_ref, n2g_ref, wfi_ref, wfo_ref, y_ref):
    out_a = _dot(o_ref[...], woa_ref[...])
    merged = (sga_ref[...] * out_a + pb_ref[...]).astype(BF16)
    x1 = x_ref[...] + g1_ref[...] * _dot(merged, wout_ref[...])
    h2 = (x1 * _rms_scale(x1, D_MODEL)) * (n2g_ref[...] * (1.0 + sc2_ref[...])) + sh2_ref[...]
    h2 = h2.astype(BF16)
    up = _dot(h2, wfi_ref[:, :D_FF])
    gate = _dot(h2, wfi_ref[:, D_FF:])
    act = (gate * _sigmoid(gate) * up).astype(BF16)
    y_ref[...] = x1 + g2_ref[...] * _dot(act, wfo_ref[...])


def _mixer_out(layer, x2d, o, sga, pb, mod5, row0, seq, w):
    n_tok = x2d.shape[0]
    tm = TM_OUT
    tps = seq // tm
    tile = lambda cols: pl.BlockSpec((tm, cols), lambda i: (i, 0))
    in_specs = [
        tile(D_MODEL), tile(N_HEADS * V_DIM), tile(D_MODEL), tile(D_MODEL),
        _mod_spec(layer, 2, row0, tps),
        _mod_spec(layer, 3, row0, tps),
        _mod_spec(layer, 4, row0, tps),
        _mod_spec(layer, 5, row0, tps),
        _const_spec((N_HEADS * V_DIM, D_MODEL), layer),
        _const_spec((D_MODEL, D_MODEL), layer),
        _const_spec((1, D_MODEL), layer),
        _const_spec((D_MODEL, 2 * D_FF), layer),
        _const_spec((D_FF, D_MODEL), layer),
    ]
    return pl.pallas_call(
        _mixer_out_kernel,
        out_shape=jax.ShapeDtypeStruct((n_tok, D_MODEL), F32),
        grid=(n_tok // tm,),
        in_specs=in_specs,
        out_specs=tile(D_MODEL),
        compiler_params=pltpu.CompilerParams(
            dimension_semantics=("arbitrary",), vmem_limit_bytes=VMEM_LIMIT),
        name="mixer_out",
    )(x2d, o, sga, pb, mod5, mod5, mod5, mod5,
      w["w_o_a"], w["w_out"], w["norm2_g"], w["w_ffn_in"], w["w_ffn_out"])


def _slot_source():
    half = QK_ROPE // 2
    src = np.full((LANES,), QK_DIM, np.int32)
    src[0:48] = np.arange(0, 48)
    src[48:48 + half] = np.arange(QK_NOPE, QK_NOPE + half)
    src[64:80] = np.arange(48, QK_NOPE)
    src[LANES - half:] = np.arange(QK_NOPE + half, QK_DIM)
    return src


def _slot_lanes(a):
    a = jnp.pad(a, [(0, 0)] * (a.ndim - 1) + [(0, 1)])
    return jnp.take(a, _slot_source(), axis=-1)


def _head_slots(w, first_dim, width):
    L, K, _ = w.shape
    w = w.reshape(L, K, N_HEADS, width)
    w = jnp.pad(w, ((0, 0), (0, 0), (0, 0), (first_dim, QK_DIM - first_dim - width)))
    return _slot_lanes(w).reshape(L, K, SLOTS)


def _rope_tables(seq):
    pos = jnp.arange(seq, dtype=F32)
    inv = ROPE_BASE ** (-jnp.arange(0, QK_ROPE, 2, dtype=F32) / QK_ROPE)
    ang = pos[:, None] * inv[None, :]
    cos, sin = jnp.cos(ang), jnp.sin(ang)
    ones = jnp.ones((seq, QK_NOPE), F32)
    zn = jnp.zeros((seq, QK_NOPE), F32)
    tc = _slot_lanes(jnp.concatenate([ones, cos, cos], axis=1))
    ts = _slot_lanes(jnp.concatenate([zn, -sin, sin], axis=1))
    return tc, ts


def _prepare_weights(norm1_g, w_in, q_a_norm_g, kv_a_norm_g, w_q_b, w_kv_b, q_norm_g, k_norm_g,
                     w_o_a, sgu_norm_g, w_s, b_s, w_o_b, w_out, norm2_g, w_ffn_in, w_ffn_out):
    L = DEPTH
    lat_end = Q_LORA + KV_LORA + QK_ROPE
    w_kr = jnp.pad(w_in[:, :, Q_LORA + KV_LORA:lat_end], ((0, 0), (0, 0), (QK_NOPE, 0)))
    w_lat = jnp.concatenate([w_in[:, :, :Q_LORA + KV_LORA], _slot_lanes(w_kr)],
                            axis=-1).astype(BF16)
    w_big = w_in[:, :, lat_end:].astype(BF16)
    kv4 = w_kv_b.reshape(L, KV_LORA, N_HEADS, QK_NOPE + V_DIM)
    w_kn = _head_slots(kv4[..., :QK_NOPE].reshape(L, KV_LORA, N_HEADS * QK_NOPE), 0, QK_NOPE)
    wv = kv4[..., QK_NOPE:]
    zeros = jnp.zeros_like(wv)
    even = jnp.concatenate([wv, zeros], axis=-1)
    odd = jnp.concatenate([zeros, wv], axis=-1)
    is_even = (jnp.arange(N_HEADS) % 2 == 0)[None, None, :, None]
    w_v = jnp.where(is_even, even, odd).reshape(L, KV_LORA, SLOTS)
    v_ones = np.zeros((SLOTS, TM_IN), np.float32)
    for hd in range(N_HEADS):
        v_ones[hd * LANES + (V_DIM if hd % 2 == 0 else 0), :] = 1.0
    q_scale = (QK_DIM ** -0.5) * math.log2(math.e)
    gq_slot = _slot_lanes(q_norm_g * q_scale).reshape(L, 1, LANES)
    gk_slot = _slot_lanes(k_norm_g).reshape(L, 1, LANES)
    return {
        "norm1_g": norm1_g.reshape(L, 1, D_MODEL),
        "w_lat": w_lat,
        "w_big": w_big,
        "q_a_norm_g": q_a_norm_g.reshape(L, 1, Q_LORA),
        "kv_a_norm_g": kv_a_norm_g.reshape(L, 1, KV_LORA),
        "w_q_b": _head_slots(w_q_b, 0, QK_DIM).astype(BF16),
        "w_kn_b": w_kn.astype(BF16),
        "w_v_t": jnp.swapaxes(w_v, 1, 2).astype(BF16),
        "gq_slot": gq_slot,
        "gq_rolled": jnp.roll(gq_slot, LANES // 2, axis=-1),
        "gk_slot": gk_slot,
        "gk_rolled": jnp.roll(gk_slot, LANES // 2, axis=-1),
        "v_ones": jnp.asarray(v_ones),
        "sgu_norm_g": sgu_norm_g.reshape(L, 1, D_MODEL),
        "w_s": w_s.astype(BF16),
        "b_s_full": jnp.repeat(jnp.swapaxes(b_s, 1, 2), CHUNK, axis=2),
        "w_o_b": w_o_b.astype(BF16),
        "w_o_a": w_o_a.astype(BF16),
        "w_out": w_out.astype(BF16),
        "norm2_g": norm2_g.reshape(L, 1, D_MODEL),
        "w_ffn_in": w_ffn_in.astype(BF16),
        "w_ffn_out": w_ffn_out.astype(BF16),
    }


def kernel(x_prompt, x_sample, c_prompt, c_sample, w_ada, b_ada, norm1_g, w_in, q_a_norm_g,
           kv_a_norm_g, w_q_b, w_kv_b, q_norm_g, k_norm_g, w_o_a, sgu_norm_g, w_s, b_s, w_o_b,
           w_out, norm2_g, w_ffn_in, w_ffn_out):
    w = _prepare_weights(norm1_g, w_in, q_a_norm_g, kv_a_norm_g, w_q_b, w_kv_b, q_norm_g,
                         k_norm_g, w_o_a, sgu_norm_g, w_s, b_s, w_o_b, w_out, norm2_g,
                         w_ffn_in, w_ffn_out)
    n_prompt, n_sample = c_prompt.shape[0], c_sample.shape[0]
    c16 = jnp.concatenate(
        [c_prompt, c_sample, jnp.zeros((MOD_ROWS - n_prompt - n_sample, D_MODEL), F32)], axis=0)
    mod = _modulation(c16, w_ada, b_ada)
    mod5 = mod.reshape(DEPTH, MOD_ROWS, N_MOD, 1, D_MODEL)

    groups = []
    for x, row0 in ((x_prompt, 0), (x_sample, n_prompt)):
        batch, seq, _ = x.shape
        groups.append([x.reshape(batch * seq, D_MODEL), row0, batch, seq, _rope_tables(seq)])

    for layer in range(DEPTH):
        for grp in groups:
            x2d, row0, batch, seq, tabs = grp
            q, k, v, sga, pb = _mixer_in(layer, x2d, mod5, row0, seq, w, tabs)
            o = _attention(q, k, v, batch, seq)
            grp[0] = _mixer_out(layer, x2d, o, sga, pb, mod5, row0, seq, w)

    return tuple(grp[0].reshape(grp[2], grp[3], D_MODEL) for grp in groups)
```

```python
import functools
import math

import jax
import jax.numpy as jnp
import numpy as np
from jax import lax
from jax.experimental import pallas as pl
from jax.experimental.pallas import tpu as pltpu

D_MODEL = 1024
DEPTH = 4
N_HEADS = 8
QK_NOPE = 64
QK_ROPE = 32
QK_DIM = QK_NOPE + QK_ROPE
V_DIM = 64
Q_LORA = 384
KV_LORA = 256
ROPE_BASE = 10000.0
CHUNK = 128
SGU_GROUPS = 8
D_FF = 2816
N_MOD = 6
EPS = 1e-6

LANES = 128
SLOTS = N_HEADS * LANES
V_ROWS = 80
V_SLOTS = N_HEADS * V_ROWS
LAT_COLS = 768
MOD_ROWS = 16
VMEM_LIMIT = 56 * 1024 * 1024

TM_IN = 256
TM_OUT = 256
TQ = 1024
TK = TM_IN
TS = 256
STEPS_PER_BODY = 4
MOD_TN = 512

F32 = jnp.float32
BF16 = jnp.bfloat16


def _dot(a, b):
    return jnp.dot(a, b, preferred_element_type=F32)


def _rms_scale(x, n):
    return lax.rsqrt(jnp.sum(x * x, axis=-1, keepdims=True) * (1.0 / n) + EPS)


def _gelu_tanh(x):
    c = math.sqrt(2.0 / math.pi)
    hx = 0.5 * x
    return hx + hx * jnp.tanh(x * (c + (c * 0.044715) * (x * x)))


def _sigmoid(x):
    return 1.0 / (1.0 + jnp.exp(-x))


def _mod_kernel(c_ref, w_ref, b_ref, o_ref):
    c = c_ref[...]
    a = (c * _sigmoid(c)).astype(BF16)
    o_ref[...] = _dot(a, w_ref[...].astype(BF16)) + b_ref[...]


def _modulation(c16, w_ada, b_ada):
    n_cols = N_MOD * D_MODEL
    return pl.pallas_call(
        _mod_kernel,
        out_shape=jax.ShapeDtypeStruct((DEPTH, MOD_ROWS, n_cols), F32),
        grid=(DEPTH, n_cols // MOD_TN),
        in_specs=[
            pl.BlockSpec((MOD_ROWS, D_MODEL), lambda l, j: (0, 0)),
            pl.BlockSpec((None, D_MODEL, MOD_TN), lambda l, j: (l, 0, j)),
            pl.BlockSpec((None, 1, MOD_TN), lambda l, j: (l, 0, j)),
        ],
        out_specs=pl.BlockSpec((None, MOD_ROWS, MOD_TN), lambda l, j: (l, 0, j)),
        compiler_params=pltpu.CompilerParams(
            dimension_semantics=("arbitrary", "arbitrary"),
            vmem_limit_bytes=VMEM_LIMIT),
        name="adaln_modulation",
    )(c16, w_ada, b_ada.reshape(DEPTH, 1, n_cols))


def _norm_rope(x, a, b):
    return _rms_scale(x, QK_DIM) * (x * a + pltpu.roll(x, LANES // 2, axis=1) * b)


def _mixer_in_kernel(x_ref, sh_ref, sc_ref, n1g_ref, wlat_ref, wbig_ref, gqa_ref, gkva_ref,
                     wqt_ref, wknb_ref, wvt_ref, gqc_ref, gqrc_ref, gk_ref, gkr_ref, vone_ref,
                     tc_ref, ts_ref, tct_ref, tst_ref, gsgu_ref, ws_ref, bs_ref, wob_ref,
                     qt_ref, k_ref, vt_ref, sga_ref, pb_ref, prod_scr):
    tm = x_ref.shape[0]
    x = x_ref[...]
    h = (x * _rms_scale(x, D_MODEL)) * (n1g_ref[...] * (1.0 + sc_ref[...])) + sh_ref[...]
    hb = h.astype(BF16)

    lat = _dot(hb, wlat_ref[...])
    q_lat = lat[:, :Q_LORA]
    kv_lat = lat[:, Q_LORA:Q_LORA + KV_LORA]
    kr_slot = lat[:, Q_LORA + KV_LORA:]
    qn = (q_lat * _rms_scale(q_lat, Q_LORA) * gqa_ref[...]).astype(BF16)
    kvn = (kv_lat * _rms_scale(kv_lat, KV_LORA) * gkva_ref[...]).astype(BF16)
    nt = (((1,), (1,)), ((), ()))
    qt_all = lax.dot_general(wqt_ref[...], qn, nt, preferred_element_type=F32)
    kn_all = _dot(kvn, wknb_ref[...])
    vt = lax.dot_general(wvt_ref[...], kvn, nt, preferred_element_type=F32)
    vt_ref[...] = (vt + vone_ref[...]).astype(BF16)

    tc, ts = tc_ref[...], ts_ref[...]
    ka, kb = tc * gk_ref[...], ts * gkr_ref[...]
    qat, qbt = tct_ref[...] * gqc_ref[...], tst_ref[...] * gqrc_ref[...]

    def qk_heads(heads):
        for hd in heads:
            sl = slice(hd * LANES, (hd + 1) * LANES)
            xt = qt_all[sl, :]
            rt = lax.rsqrt(jnp.sum(xt * xt, axis=0, keepdims=True) * (1.0 / QK_DIM) + EPS)
            swapped = jnp.concatenate([xt[LANES // 2:], xt[:LANES // 2]], axis=0)
            qt_ref[sl, :] = (rt * (xt * qat + swapped * qbt)).astype(BF16)
            k_ref[:, sl] = _norm_rope(kn_all[:, sl] + kr_slot, ka, kb).astype(BF16)

    zu = _dot(hb, wbig_ref[:, 0:D_MODEL])
    qk_heads(range(0, 2))
    zv = _dot(hb, wbig_ref[:, D_MODEL:2 * D_MODEL])
    qk_heads(range(2, 4))
    u = _gelu_tanh(zu)
    za = _dot(hb, wbig_ref[:, 2 * D_MODEL:3 * D_MODEL])
    qk_heads(range(4, 6))
    gv = _gelu_tanh(zv)
    vn = (gv * _rms_scale(gv, D_MODEL) * gsgu_ref[...]).astype(BF16)
    zb = _dot(hb, wbig_ref[:, 3 * D_MODEL:4 * D_MODEL])
    qk_heads(range(6, 8))

    for c in range(0, tm // CHUNK, 2):
        rows0 = slice(c * CHUNK, (c + 1) * CHUNK)
        rows1 = slice((c + 1) * CHUNK, (c + 2) * CHUNK)
        for g in range(SGU_GROUPS):
            cols = slice(g * CHUNK, (g + 1) * CHUNK)
            pair = jnp.concatenate([vn[rows0, cols], vn[rows1, cols]], axis=1)
            mixed = _dot(ws_ref[g], pair)
            bias = bs_ref[:, cols]
            prod_scr[rows0, cols] = (u[rows0, cols] * (mixed[:, :CHUNK] + bias)).astype(BF16)
            prod_scr[rows1, cols] = (u[rows1, cols] * (mixed[:, CHUNK:] + bias)).astype(BF16)
    sga_ref[...] = _sigmoid(za)
    out_b = _dot(prod_scr[...], wob_ref[...])
    pb_ref[...] = _sigmoid(zb) * out_b


def _const_spec(shape, layer=None):
    if layer is None:
        return pl.BlockSpec(shape, lambda i: (0,) * len(shape), pipeline_mode=pl.Buffered(1))
    return pl.BlockSpec((None,) + shape, lambda i: (layer,) + (0,) * len(shape),
                        pipeline_mode=pl.Buffered(1))


def _mod_spec(layer, which, row0, tiles_per_seq):
    return pl.BlockSpec((None, None, None, 1, D_MODEL),
                        lambda i: (layer, row0 + i // tiles_per_seq, which, 0, 0))


def _mixer_in(layer, x2d, mod5, row0, seq, w, tabs):
    n_tok = x2d.shape[0]
    tm = TM_IN
    tps = seq // tm
    tile = lambda cols: pl.BlockSpec((tm, cols), lambda i: (i, 0))
    tab_spec = pl.BlockSpec((tm, LANES), lambda i: (i % tps, 0))
    tabt_spec = pl.BlockSpec((LANES, tm), lambda i: (0, i % tps))
    in_specs = [
        tile(D_MODEL),
        _mod_spec(layer, 0, row0, tps),
        _mod_spec(layer, 1, row0, tps),
        _const_spec((1, D_MODEL), layer),
        _const_spec((D_MODEL, LAT_COLS), layer),
        _const_spec((D_MODEL, 4 * D_MODEL), layer),
        _const_spec((1, Q_LORA), layer),
        _const_spec((1, KV_LORA), layer),
        _const_spec((SLOTS, Q_LORA), layer),
        _const_spec((KV_LORA, SLOTS), layer),
        _const_spec((V_SLOTS, KV_LORA), layer),
        _const_spec((LANES, 1), layer),
        _const_spec((LANES, 1), layer),
        _const_spec((1, LANES), layer),
        _const_spec((1, LANES), layer),
        _const_spec((V_SLOTS, tm)),
        tab_spec, tab_spec, tabt_spec, tabt_spec,
        _const_spec((1, D_MODEL), layer),
        _const_spec((SGU_GROUPS, CHUNK, CHUNK), layer),
        _const_spec((CHUNK, D_MODEL), layer),
        _const_spec((D_MODEL, D_MODEL), layer),
    ]
    out_shape = (
        jax.ShapeDtypeStruct((SLOTS, n_tok), BF16),
        jax.ShapeDtypeStruct((n_tok, SLOTS), BF16),
        jax.ShapeDtypeStruct((n_tok // tm, V_SLOTS, tm), BF16),
        jax.ShapeDtypeStruct((n_tok, D_MODEL), F32),
        jax.ShapeDtypeStruct((n_tok, D_MODEL), F32),
    )
    out_specs = (pl.BlockSpec((SLOTS, tm), lambda i: (0, i)), tile(SLOTS),
                 pl.BlockSpec((None, V_SLOTS, tm), lambda i: (i, 0, 0)),
                 tile(D_MODEL), tile(D_MODEL))
    return pl.pallas_call(
        _mixer_in_kernel,
        out_shape=out_shape,
        grid=(n_tok // tm,),
        in_specs=in_specs,
        out_specs=out_specs,
        scratch_shapes=[pltpu.VMEM((tm, D_MODEL), BF16)],
        compiler_params=pltpu.CompilerParams(
            dimension_semantics=("arbitrary",), vmem_limit_bytes=VMEM_LIMIT),
        name="mixer_in",
    )(x2d, mod5, mod5, w["norm1_g"], w["w_lat"], w["w_big"], w["q_a_norm_g"], w["kv_a_norm_g"],
      w["w_q_t"], w["w_kn_b"], w["w_v_t"], w["gq_col"], w["gq_rolled_col"], w["gk_slot"],
      w["gk_rolled"], w["v_ones"], tabs[0], tabs[1], tabs[2], tabs[3], w["sgu_norm_g"], w["w_s"],
      w["b_s_full"], w["w_o_b"])


def _attn_kernel(qt_ref, k_ref, vt_ref, o_ref, m_scr, acc_scr, *parity_bufs):
    tq = qt_ref.shape[1]
    n_chunks = vt_ref.shape[0]
    m_scr[...] = jnp.full(m_scr.shape, -jnp.inf, F32)
    acc_scr[...] = jnp.zeros(acc_scr.shape, F32)
    st_scr, cm_scr = parity_bufs[0::2], parity_bufs[1::2]

    items = [(hh, sub) for hh in range(2) for sub in range(tq // TS)]

    def scores(item, j, par):
        hh, sub = item
        sl = slice(hh * LANES, (hh + 1) * LANES)
        cs = slice(sub * TS, (sub + 1) * TS)
        kc = k_ref[pl.ds(pl.multiple_of(j * TK, TK), TK), sl]
        st = _dot(kc, qt_ref[sl, cs])
        st_scr[par][hh, :, cs] = st
        cm_scr[par][hh, :, cs] = jnp.max(st, axis=0, keepdims=True)

    def probs(item, par):
        hh, sub = item
        cs = slice(sub * TS, (sub + 1) * TS)
        m_old = m_scr[hh, :, cs]
        m_new = jnp.maximum(m_old, cm_scr[par][hh, :, cs])
        m_scr[hh, :, cs] = m_new
        pt = jnp.exp2(st_scr[par][hh, :, cs] - m_new).astype(BF16)
        return pt, jnp.exp2(m_old - m_new)

    def accumulate(item, j, pt, alpha):
        hh, sub = item
        rows = slice(hh * V_ROWS, (hh + 1) * V_ROWS)
        cs = slice(sub * TS, (sub + 1) * TS)
        acc_scr[hh, :, cs] = alpha * acc_scr[hh, :, cs] + _dot(vt_ref[j, rows, :], pt)

    def step(j, par, do_scores=True):
        for item in items:
            pt, alpha = probs(item, par)
            if do_scores:
                scores(item, j + 1, 1 - par)
            accumulate(item, j, pt, alpha)

    for item in items:
        scores(item, 0, 0)

    spb = STEPS_PER_BODY if n_chunks > 2 * STEPS_PER_BODY else 2

    def body(i, carry):
        for s in range(spb):
            step(spb * i + s, s % 2)
        return carry

    full = (n_chunks - 1) // spb
    lax.fori_loop(0, full, body, 0)
    for j in range(spb * full, n_chunks - 1):
        step(j, j % 2)
    step(n_chunks - 1, (n_chunks - 1) % 2, do_scores=False)
    outs = []
    for hh in range(2):
        acc = acc_scr[hh]
        outs.append(acc[:V_DIM, :] * (1.0 / acc[V_DIM:V_DIM + 1, :]))
    o_ref[...] = jnp.concatenate(outs, axis=0).T.astype(BF16)


def _attention(qt, k, vt, batch, seq):
    n_tok = k.shape[0]
    tq = seq if seq <= 2 * TQ else TQ
    n_qt = seq // tq
    pair = 2 * LANES
    return pl.pallas_call(
        _attn_kernel,
        out_shape=jax.ShapeDtypeStruct((n_tok, N_HEADS * V_DIM), BF16),
        grid=(batch, N_HEADS // 2, n_qt),
        in_specs=[
            pl.BlockSpec((pair, tq), lambda b, hp, i: (hp, b * n_qt + i)),
            pl.BlockSpec((seq, pair), lambda b, hp, i: (b, hp)),
            pl.BlockSpec((seq // TK, 2 * V_ROWS, TK), lambda b, hp, i: (b, hp, 0)),
        ],
        out_specs=pl.BlockSpec((tq, LANES), lambda b, hp, i: (b * n_qt + i, hp)),
        scratch_shapes=[
            pltpu.VMEM((2, 1, tq), F32),
            pltpu.VMEM((2, V_ROWS, tq), F32),
        ] + 2 * [
            pltpu.VMEM((2, TK, tq), F32),
            pltpu.VMEM((2, 1, tq), F32),
        ],
        compiler_params=pltpu.CompilerParams(
            dimension_semantics=("arbitrary", "arbitrary", "arbitrary"),
            vmem_limit_bytes=VMEM_LIMIT),
        name="attention",
    )(qt, k, vt)


def _mixer_out_kernel(x_ref, o_ref, sga_ref, pb_ref, g1_ref, sh2_ref, sc2_ref, g2_ref,
                      woa_ref, wout_ref, n2g_ref, wfi_ref, wfo_ref, y_ref):
    out_a = _dot(o_ref[...], woa_ref[...])
    merged = (sga_ref[...] * out_a + pb_ref[...]).astype(BF16)
    x1 = x_ref[...] + g1_ref[...] * _dot(merged, wout_ref[...])
    h2 = (x1 * _rms_scale(x1, D_MODEL)) * (n2g_ref[...] * (1.0 + sc2_ref[...])) + sh2_ref[...]
    h2 = h2.astype(BF16)
    up = _dot(h2, wfi_ref[:, :D_FF])
    gate = _dot(h2, wfi_ref[:, D_FF:])
    act = (gate * _sigmoid(gate) * up).astype(BF16)
    y_ref[...] = x1 + g2_ref[...] * _dot(act, wfo_ref[...])


def _mixer_out(layer, x2d, o, sga, pb, mod5, row0, seq, w):
    n_tok = x2d.shape[0]
    tm = TM_OUT
    tps = seq // tm
    tile = lambda cols: pl.BlockSpec((tm, cols), lambda i: (i, 0))
    in_specs = [
        tile(D_MODEL), tile(N_HEADS * V_DIM), tile(D_MODEL), tile(D_MODEL),
        _mod_spec(layer, 2, row0, tps),
        _mod_spec(layer, 3, row0, tps),
        _mod_spec(layer, 4, row0, tps),
        _mod_spec(layer, 5, row0, tps),
        _const_spec((N_HEADS * V_DIM, D_MODEL), layer),
        _const_spec((D_MODEL, D_MODEL), layer),
        _const_spec((1, D_MODEL), layer),
        _const_spec((D_MODEL, 2 * D_FF), layer),
        _const_spec((D_FF, D_MODEL), layer),
    ]
    return pl.pallas_call(
        _mixer_out_kernel,
        out_shape=jax.ShapeDtypeStruct((n_tok, D_MODEL), F32),
        grid=(n_tok // tm,),
        in_specs=in_specs,
        out_specs=tile(D_MODEL),
        compiler_params=pltpu.CompilerParams(
            dimension_semantics=("arbitrary",), vmem_limit_bytes=VMEM_LIMIT),
        name="mixer_out",
    )(x2d, o, sga, pb, mod5, mod5, mod5, mod5,
      w["w_o_a"], w["w_out"], w["norm2_g"], w["w_ffn_in"], w["w_ffn_out"])


def _slot_source():
    half = QK_ROPE // 2
    src = np.full((LANES,), QK_DIM, np.int32)
    src[0:48] = np.arange(0, 48)
    src[48:48 + half] = np.arange(QK_NOPE, QK_NOPE + half)
    src[64:80] = np.arange(48, QK_NOPE)
    src[LANES - half:] = np.arange(QK_NOPE + half, QK_DIM)
    return src


def _slot_lanes(a):
    a = jnp.pad(a, [(0, 0)] * (a.ndim - 1) + [(0, 1)])
    return jnp.take(a, _slot_source(), axis=-1)


def _head_slots(w, first_dim, width):
    L, K, _ = w.shape
    w = w.reshape(L, K, N_HEADS, width)
    w = jnp.pad(w, ((0, 0), (0, 0), (0, 0), (first_dim, QK_DIM - first_dim - width)))
    return _slot_lanes(w).reshape(L, K, SLOTS)


def _rope_tables(seq):
    pos = jnp.arange(seq, dtype=F32)
    inv = ROPE_BASE ** (-jnp.arange(0, QK_ROPE, 2, dtype=F32) / QK_ROPE)
    ang = pos[:, None] * inv[None, :]
    cos, sin = jnp.cos(ang), jnp.sin(ang)
    ones = jnp.ones((seq, QK_NOPE), F32)
    zn = jnp.zeros((seq, QK_NOPE), F32)
    tc = _slot_lanes(jnp.concatenate([ones, cos, cos], axis=1))
    ts = _slot_lanes(jnp.concatenate([zn, -sin, sin], axis=1))
    return tc, ts, tc.T, ts.T


def _prepare_weights(norm1_g, w_in, q_a_norm_g, kv_a_norm_g, w_q_b, w_kv_b, q_norm_g, k_norm_g,
                     w_o_a, sgu_norm_g, w_s, b_s, w_o_b, w_out, norm2_g, w_ffn_in, w_ffn_out):
    L = DEPTH
    lat_end = Q_LORA + KV_LORA + QK_ROPE
    w_kr = jnp.pad(w_in[:, :, Q_LORA + KV_LORA:lat_end], ((0, 0), (0, 0), (QK_NOPE, 0)))
    w_lat = jnp.concatenate([w_in[:, :, :Q_LORA + KV_LORA], _slot_lanes(w_kr)],
                            axis=-1).astype(BF16)
    w_big = w_in[:, :, lat_end:].astype(BF16)
    kv4 = w_kv_b.reshape(L, KV_LORA, N_HEADS, QK_NOPE + V_DIM)
    w_kn = _head_slots(kv4[..., :QK_NOPE].reshape(L, KV_LORA, N_HEADS * QK_NOPE), 0, QK_NOPE)
    w_v = jnp.pad(kv4[..., QK_NOPE:], ((0, 0), (0, 0), (0, 0), (0, V_ROWS - V_DIM)))
    w_v = w_v.reshape(L, KV_LORA, V_SLOTS)
    v_ones = np.zeros((V_SLOTS, TM_IN), np.float32)
    for hd in range(N_HEADS):
        v_ones[hd * V_ROWS + V_DIM, :] = 1.0
    q_scale = (QK_DIM ** -0.5) * math.log2(math.e)
    gq_slot = _slot_lanes(q_norm_g * q_scale).reshape(L, 1, LANES)
    gk_slot = _slot_lanes(k_norm_g).reshape(L, 1, LANES)
    return {
        "norm1_g": norm1_g.reshape(L, 1, D_MODEL),
        "w_lat": w_lat,
        "w_big": w_big,
        "q_a_norm_g": q_a_norm_g.reshape(L, 1, Q_LORA),
        "kv_a_norm_g": kv_a_norm_g.reshape(L, 1, KV_LORA),
        "w_q_t": jnp.swapaxes(_head_slots(w_q_b, 0, QK_DIM), 1, 2).astype(BF16),
        "w_kn_b": w_kn.astype(BF16),
        "w_v_t": jnp.swapaxes(w_v, 1, 2).astype(BF16),
        "gq_col": jnp.swapaxes(gq_slot, 1, 2),
        "gq_rolled_col": jnp.swapaxes(jnp.roll(gq_slot, LANES // 2, axis=-1), 1, 2),
        "gk_slot": gk_slot,
        "gk_rolled": jnp.roll(gk_slot, LANES // 2, axis=-1),
        "v_ones": jnp.asarray(v_ones),
        "sgu_norm_g": sgu_norm_g.reshape(L, 1, D_MODEL),
        "w_s": w_s.astype(BF16),
        "b_s_full": jnp.repeat(jnp.swapaxes(b_s, 1, 2), CHUNK, axis=2),
        "w_o_b": w_o_b.astype(BF16),
        "w_o_a": w_o_a.astype(BF16),
        "w_out": w_out.astype(BF16),
        "norm2_g": norm2_g.reshape(L, 1, D_MODEL),
        "w_ffn_in": w_ffn_in.astype(BF16),
        "w_ffn_out": w_ffn_out.astype(BF16),
    }


def kernel(x_prompt, x_sample, c_prompt, c_sample, w_ada, b_ada, norm1_g, w_in, q_a_norm_g,
           kv_a_norm_g, w_q_b, w_kv_b, q_norm_g, k_norm_g, w_o_a, sgu_norm_g, w_s, b_s, w_o_b,
           w_out, norm2_g, w_ffn_in, w_ffn_out):
    w = _prepare_weights(norm1_g, w_in, q_a_norm_g, kv_a_norm_g, w_q_b, w_kv_b, q_norm_g,
                         k_norm_g, w_o_a, sgu_norm_g, w_s, b_s, w_o_b, w_out, norm2_g,
                         w_ffn_in, w_ffn_out)
    n_prompt, n_sample = c_prompt.shape[0], c_sample.shape[0]
    c16 = jnp.concatenate(
        [c_prompt, c_sample, jnp.zeros((MOD_ROWS - n_prompt - n_sample, D_MODEL), F32)], axis=0)
    mod = _modulation(c16, w_ada, b_ada)
    mod5 = mod.reshape(DEPTH, MOD_ROWS, N_MOD, 1, D_MODEL)

    groups = []
    for x, row0 in ((x_prompt, 0), (x_sample, n_prompt)):
        batch, seq, _ = x.shape
        groups.append([x.reshape(batch * seq, D_MODEL), row0, batch, seq, _rope_tables(seq)])

    for layer in range(DEPTH):
        for grp in groups:
            x2d, row0, batch, seq, tabs = grp
            qt, k, vt, sga, pb = _mixer_in(layer, x2d, mod5, row0, seq, w, tabs)
            o = _attention(qt, k, vt, batch, seq)
            grp[0] = _mixer_out(layer, x2d, o, sga, pb, mod5, row0, seq, w)

    return tuple(grp[0].reshape(grp[2], grp[3], D_MODEL) for grp in groups)
```

```python
import functools
import math

import jax
import jax.numpy as jnp
import numpy as np
from jax import lax
from jax.experimental import pallas as pl
from jax.experimental.pallas import tpu as pltpu

D_MODEL = 1024
DEPTH = 4
N_HEADS = 8
QK_NOPE = 64
QK_ROPE = 32
QK_DIM = QK_NOPE + QK_ROPE
V_DIM = 64
Q_LORA = 384
KV_LORA = 256
ROPE_BASE = 10000.0
CHUNK = 128
SGU_GROUPS = 8
D_FF = 2816
N_MOD = 6
EPS = 1e-6

LANES = 128
SLOTS = N_HEADS * LANES
V_ROWS = 80
V_SLOTS = N_HEADS * V_ROWS
LAT_COLS = 768
MOD_ROWS = 16
VMEM_LIMIT = 56 * 1024 * 1024

TM_IN = 256
TM_OUT = 256
TQ = 1024
TK = TM_IN
TS = 256
STEPS_PER_BODY = 4
MOD_TN = 512

F32 = jnp.float32
BF16 = jnp.bfloat16


def _dot(a, b):
    return jnp.dot(a, b, preferred_element_type=F32)


def _rms_scale(x, n):
    return lax.rsqrt(jnp.sum(x * x, axis=-1, keepdims=True) * (1.0 / n) + EPS)


def _gelu_tanh(x):
    c = math.sqrt(2.0 / math.pi)
    hx = 0.5 * x
    return hx + hx * jnp.tanh(x * (c + (c * 0.044715) * (x * x)))


def _sigmoid(x):
    return 1.0 / (1.0 + jnp.exp(-x))


def _mod_kernel(c_ref, w_ref, b_ref, o_ref):
    c = c_ref[...]
    a = (c * _sigmoid(c)).astype(BF16)
    o_ref[...] = _dot(a, w_ref[...].astype(BF16)) + b_ref[...]


def _modulation(c16, w_ada, b_ada):
    n_cols = N_MOD * D_MODEL
    return pl.pallas_call(
        _mod_kernel,
        out_shape=jax.ShapeDtypeStruct((DEPTH, MOD_ROWS, n_cols), F32),
        grid=(DEPTH, n_cols // MOD_TN),
        in_specs=[
            pl.BlockSpec((MOD_ROWS, D_MODEL), lambda l, j: (0, 0)),
            pl.BlockSpec((None, D_MODEL, MOD_TN), lambda l, j: (l, 0, j)),
            pl.BlockSpec((None, 1, MOD_TN), lambda l, j: (l, 0, j)),
        ],
        out_specs=pl.BlockSpec((None, MOD_ROWS, MOD_TN), lambda l, j: (l, 0, j)),
        compiler_params=pltpu.CompilerParams(
            dimension_semantics=("arbitrary", "arbitrary"),
            vmem_limit_bytes=VMEM_LIMIT),
        name="adaln_modulation",
    )(c16, w_ada, b_ada.reshape(DEPTH, 1, n_cols))


def _norm_rope(x, a, b):
    return _rms_scale(x, QK_DIM) * (x * a + pltpu.roll(x, LANES // 2, axis=1) * b)


def _mixer_in_kernel(x_ref, sh_ref, sc_ref, n1g_ref, wall_ref, gqa_ref, gkva_ref,
                     wqt_ref, wknb_ref, wvt_ref, gqc_ref, gqrc_ref, gk_ref, gkr_ref, vone_ref,
                     tc_ref, ts_ref, tct_ref, tst_ref, gsgu_ref, ws_ref, bs_ref, wob_ref,
                     qt_ref, k_ref, vt_ref, sga_ref, pb_ref, prod_scr):
    tm = x_ref.shape[0]
    x = x_ref[...]
    h = (x * _rms_scale(x, D_MODEL)) * (n1g_ref[...] * (1.0 + sc_ref[...])) + sh_ref[...]
    hb = h.astype(BF16)

    wbig_ref = wall_ref.at[:, LAT_COLS:]
    lat = _dot(hb, wall_ref[:, :LAT_COLS])
    q_lat = lat[:, :Q_LORA]
    kv_lat = lat[:, Q_LORA:Q_LORA + KV_LORA]
    kr_slot = lat[:, Q_LORA + KV_LORA:]
    qn = (q_lat * _rms_scale(q_lat, Q_LORA) * gqa_ref[...]).astype(BF16)
    kvn = (kv_lat * _rms_scale(kv_lat, KV_LORA) * gkva_ref[...]).astype(BF16)
    nt = (((1,), (1,)), ((), ()))
    qt_all = lax.dot_general(wqt_ref[...], qn, nt, preferred_element_type=F32)
    kn_all = _dot(kvn, wknb_ref[...])
    vt = lax.dot_general(wvt_ref[...], kvn, nt, preferred_element_type=F32)
    vt_ref[...] = (vt + vone_ref[...]).astype(BF16)

    tc, ts = tc_ref[...], ts_ref[...]
    ka, kb = tc * gk_ref[...], ts * gkr_ref[...]
    qat, qbt = tct_ref[...] * gqc_ref[...], tst_ref[...] * gqrc_ref[...]

    def qk_heads(heads):
        for hd in heads:
            sl = slice(hd * LANES, (hd + 1) * LANES)
            xt = qt_all[sl, :]
            rt = lax.rsqrt(jnp.sum(xt * xt, axis=0, keepdims=True) * (1.0 / QK_DIM) + EPS)
            swapped = jnp.concatenate([xt[LANES // 2:], xt[:LANES // 2]], axis=0)
            qt_ref[sl, :] = (rt * (xt * qat + swapped * qbt)).astype(BF16)
            k_ref[:, sl] = _norm_rope(kn_all[:, sl] + kr_slot, ka, kb).astype(BF16)

    zu = _dot(hb, wbig_ref[:, 0:D_MODEL])
    qk_heads(range(0, 2))
    zv = _dot(hb, wbig_ref[:, D_MODEL:2 * D_MODEL])
    qk_heads(range(2, 4))
    u = _gelu_tanh(zu)
    za = _dot(hb, wbig_ref[:, 2 * D_MODEL:3 * D_MODEL])
    qk_heads(range(4, 6))
    gv = _gelu_tanh(zv)
    vn = (gv * _rms_scale(gv, D_MODEL) * gsgu_ref[...]).astype(BF16)
    zb = _dot(hb, wbig_ref[:, 3 * D_MODEL:4 * D_MODEL])
    qk_heads(range(6, 8))

    for c in range(0, tm // CHUNK, 2):
        rows0 = slice(c * CHUNK, (c + 1) * CHUNK)
        rows1 = slice((c + 1) * CHUNK, (c + 2) * CHUNK)
        for g in range(SGU_GROUPS):
            cols = slice(g * CHUNK, (g + 1) * CHUNK)
            pair = jnp.concatenate([vn[rows0, cols], vn[rows1, cols]], axis=1)
            mixed = _dot(ws_ref[g], pair)
            bias = bs_ref[:, cols]
            prod_scr[rows0, cols] = (u[rows0, cols] * (mixed[:, :CHUNK] + bias)).astype(BF16)
            prod_scr[rows1, cols] = (u[rows1, cols] * (mixed[:, CHUNK:] + bias)).astype(BF16)
    sga_ref[...] = _sigmoid(za)
    out_b = _dot(prod_scr[...], wob_ref[...])
    pb_ref[...] = _sigmoid(zb) * out_b


def _const_spec(shape, layer=None):
    if layer is None:
        return pl.BlockSpec(shape, lambda i: (0,) * len(shape), pipeline_mode=pl.Buffered(1))
    return pl.BlockSpec((None,) + shape, lambda i: (layer,) + (0,) * len(shape),
                        pipeline_mode=pl.Buffered(1))


def _mod_spec(layer, which, row0, tiles_per_seq):
    return pl.BlockSpec((None, None, None, 1, D_MODEL),
                        lambda i: (layer, row0 + i // tiles_per_seq, which, 0, 0))


def _mixer_in(layer, x2d, mod5, row0, seq, w, tabs):
    n_tok = x2d.shape[0]
    tm = TM_IN
    tps = seq // tm
    tile = lambda cols: pl.BlockSpec((tm, cols), lambda i: (i, 0))
    tab_spec = pl.BlockSpec((tm, LANES), lambda i: (i % tps, 0))
    tabt_spec = pl.BlockSpec((LANES, tm), lambda i: (0, i % tps))
    in_specs = [
        tile(D_MODEL),
        _mod_spec(layer, 0, row0, tps),
        _mod_spec(layer, 1, row0, tps),
        _const_spec((1, D_MODEL), layer),
        _const_spec((D_MODEL, LAT_COLS + 4 * D_MODEL), layer),
        _const_spec((1, Q_LORA), layer),
        _const_spec((1, KV_LORA), layer),
        _const_spec((SLOTS, Q_LORA), layer),
        _const_spec((KV_LORA, SLOTS), layer),
        _const_spec((V_SLOTS, KV_LORA), layer),
        _const_spec((LANES, 1), layer),
        _const_spec((LANES, 1), layer),
        _const_spec((1, LANES), layer),
        _const_spec((1, LANES), layer),
        _const_spec((V_SLOTS, tm)),
        tab_spec, tab_spec, tabt_spec, tabt_spec,
        _const_spec((1, D_MODEL), layer),
        _const_spec((SGU_GROUPS, CHUNK, CHUNK), layer),
        _const_spec((CHUNK, D_MODEL), layer),
        _const_spec((D_MODEL, D_MODEL), layer),
    ]
    out_shape = (
        jax.ShapeDtypeStruct((SLOTS, n_tok), BF16),
        jax.ShapeDtypeStruct((n_tok, SLOTS), BF16),
        jax.ShapeDtypeStruct((n_tok // tm, V_SLOTS, tm), BF16),
        jax.ShapeDtypeStruct((n_tok, D_MODEL), F32),
        jax.ShapeDtypeStruct((n_tok, D_MODEL), F32),
    )
    out_specs = (pl.BlockSpec((SLOTS, tm), lambda i: (0, i)), tile(SLOTS),
                 pl.BlockSpec((None, V_SLOTS, tm), lambda i: (i, 0, 0)),
                 tile(D_MODEL), tile(D_MODEL))
    return pl.pallas_call(
        _mixer_in_kernel,
        out_shape=out_shape,
        grid=(n_tok // tm,),
        in_specs=in_specs,
        out_specs=out_specs,
        scratch_shapes=[pltpu.VMEM((tm, D_MODEL), BF16)],
        compiler_params=pltpu.CompilerParams(
            dimension_semantics=("arbitrary",), vmem_limit_bytes=VMEM_LIMIT),
        name="mixer_in",
    )(x2d, mod5, mod5, w["norm1_g"], w["w_all"], w["q_a_norm_g"], w["kv_a_norm_g"],
      w["w_q_t"], w["w_kn_b"], w["w_v_t"], w["gq_col"], w["gq_rolled_col"], w["gk_slot"],
      w["gk_rolled"], w["v_ones"], tabs[0], tabs[1], tabs[2], tabs[3], w["sgu_norm_g"], w["w_s"],
      w["b_s_full"], w["w_o_b"])


def _attn_kernel(qt_ref, k_ref, vt_ref, o_ref, m_scr, acc_scr, *parity_bufs):
    tq = qt_ref.shape[1]
    n_chunks = vt_ref.shape[0]
    m_scr[...] = jnp.full(m_scr.shape, -jnp.inf, F32)
    acc_scr[...] = jnp.zeros(acc_scr.shape, F32)
    st_scr, cm_scr = parity_bufs[0::2], parity_bufs[1::2]

    n_sub = tq // TS
    items = [(hh, sub) for hh in range(2) for sub in range(n_sub)]

    def scores(item, j, par):
        hh, sub = item
        it = hh * n_sub + sub
        sl = slice(hh * LANES, (hh + 1) * LANES)
        kc = k_ref[pl.ds(pl.multiple_of(j * TK, TK), TK), sl]
        st = _dot(kc, qt_ref[sl, sub * TS:(sub + 1) * TS])
        st_scr[par][it] = st
        cm_scr[par][it] = jnp.max(st, axis=0, keepdims=True)

    def probs(item, par):
        hh, sub = item
        it = hh * n_sub + sub
        m_old = m_scr[it]
        m_new = jnp.maximum(m_old, cm_scr[par][it])
        m_scr[it] = m_new
        pt = jnp.exp2(st_scr[par][it] - m_new).astype(BF16)
        return pt, jnp.exp2(m_old - m_new)

    def accumulate(item, j, pt, alpha):
        hh, sub = item
        it = hh * n_sub + sub
        rows = slice(hh * V_ROWS, (hh + 1) * V_ROWS)
        acc_scr[it] = alpha * acc_scr[it] + _dot(vt_ref[j, rows, :], pt)

    def step(j, par, do_scores=True):
        for item in items:
            pt, alpha = probs(item, par)
            if do_scores:
                scores(item, j + 1, 1 - par)
            accumulate(item, j, pt, alpha)

    for item in items:
        scores(item, 0, 0)

    spb = STEPS_PER_BODY if n_chunks > 2 * STEPS_PER_BODY else 2

    def body(i, carry):
        for s in range(spb):
            step(spb * i + s, s % 2)
        return carry

    full = (n_chunks - 1) // spb
    lax.fori_loop(0, full, body, 0)
    for j in range(spb * full, n_chunks - 1):
        step(j, j % 2)
    step(n_chunks - 1, (n_chunks - 1) % 2, do_scores=False)
    outs = []
    for hh in range(2):
        acc = jnp.concatenate([acc_scr[hh * n_sub + sub] for sub in range(n_sub)], axis=1)
        outs.append(acc[:V_DIM, :] * (1.0 / acc[V_DIM:V_DIM + 1, :]))
    o_ref[...] = jnp.concatenate(outs, axis=0).T.astype(BF16)


def _attention(qt, k, vt, batch, seq):
    n_tok = k.shape[0]
    tq = seq if seq <= 2 * TQ else TQ
    n_qt = seq // tq
    n_items = 2 * (tq // TS)
    pair = 2 * LANES
    return pl.pallas_call(
        _attn_kernel,
        out_shape=jax.ShapeDtypeStruct((n_tok, N_HEADS * V_DIM), BF16),
        grid=(batch, N_HEADS // 2, n_qt),
        in_specs=[
            pl.BlockSpec((pair, tq), lambda b, hp, i: (hp, b * n_qt + i)),
            pl.BlockSpec((seq, pair), lambda b, hp, i: (b, hp)),
            pl.BlockSpec((seq // TK, 2 * V_ROWS, TK), lambda b, hp, i: (b, hp, 0)),
        ],
        out_specs=pl.BlockSpec((tq, LANES), lambda b, hp, i: (b * n_qt + i, hp)),
        scratch_shapes=[
            pltpu.VMEM((n_items, 1, TS), F32),
            pltpu.VMEM((n_items, V_ROWS, TS), F32),
        ] + 2 * [
            pltpu.VMEM((n_items, TK, TS), F32),
            pltpu.VMEM((n_items, 1, TS), F32),
        ],
        compiler_params=pltpu.CompilerParams(
            dimension_semantics=("arbitrary", "arbitrary", "arbitrary"),
            vmem_limit_bytes=VMEM_LIMIT),
        name="attention",
    )(qt, k, vt)


def _mixer_out_kernel(x_ref, o_ref, sga_ref, pb_ref, g1_ref, sh2_ref, sc2_ref, g2_ref,
                      woa_ref, wout_ref, n2g_ref, wfi_ref, wfo_ref, y_ref):
    out_a = _dot(o_ref[...], woa_ref[...])
    merged = (sga_ref[...] * out_a + pb_ref[...]).astype(BF16)
    x1 = x_ref[...] + g1_ref[...] * _dot(merged, wout_ref[...])
    h2 = (x1 * _rms_scale(x1, D_MODEL)) * (n2g_ref[...] * (1.0 + sc2_ref[...])) + sh2_ref[...]
    h2 = h2.astype(BF16)
    up = _dot(h2, wfi_ref[:, :D_FF])
    gate = _dot(h2, wfi_ref[:, D_FF:])
    act = (gate * _sigmoid(gate) * up).astype(BF16)
    y_ref[...] = x1 + g2_ref[...] * _dot(act, wfo_ref[...])


def _mixer_out(layer, x2d, o, sga, pb, mod5, row0, seq, w):
    n_tok = x2d.shape[0]
    tm = TM_OUT
    tps = seq // tm
    tile = lambda cols: pl.BlockSpec((tm, cols), lambda i: (i, 0))
    in_specs = [
        tile(D_MODEL), tile(N_HEADS * V_DIM), tile(D_MODEL), tile(D_MODEL),
        _mod_spec(layer, 2, row0, tps),
        _mod_spec(layer, 3, row0, tps),
        _mod_spec(layer, 4, row0, tps),
        _mod_spec(layer, 5, row0, tps),
        _const_spec((N_HEADS * V_DIM, D_MODEL), layer),
        _const_spec((D_MODEL, D_MODEL), layer),
        _const_spec((1, D_MODEL), layer),
        _const_spec((D_MODEL, 2 * D_FF), layer),
        _const_spec((D_FF, D_MODEL), layer),
    ]
    return pl.pallas_call(
        _mixer_out_kernel,
        out_shape=jax.ShapeDtypeStruct((n_tok, D_MODEL), F32),
        grid=(n_tok // tm,),
        in_specs=in_specs,
        out_specs=tile(D_MODEL),
        compiler_params=pltpu.CompilerParams(
            dimension_semantics=("arbitrary",), vmem_limit_bytes=VMEM_LIMIT),
        name="mixer_out",
    )(x2d, o, sga, pb, mod5, mod5, mod5, mod5,
      w["w_o_a"], w["w_out"], w["norm2_g"], w["w_ffn_in"], w["w_ffn_out"])


def _slot_source():
    half = QK_ROPE // 2
    src = np.full((LANES,), QK_DIM, np.int32)
    src[0:48] = np.arange(0, 48)
    src[48:48 + half] = np.arange(QK_NOPE, QK_NOPE + half)
    src[64:80] = np.arange(48, QK_NOPE)
    src[LANES - half:] = np.arange(QK_NOPE + half, QK_DIM)
    return src


def _slot_lanes(a):
    a = jnp.pad(a, [(0, 0)] * (a.ndim - 1) + [(0, 1)])
    return jnp.take(a, _slot_source(), axis=-1)


def _head_slots(w, first_dim, width):
    L, K, _ = w.shape
    w = w.reshape(L, K, N_HEADS, width)
    w = jnp.pad(w, ((0, 0), (0, 0), (0, 0), (first_dim, QK_DIM - first_dim - width)))
    return _slot_lanes(w).reshape(L, K, SLOTS)


def _rope_tables(seq):
    pos = jnp.arange(seq, dtype=F32)
    inv = ROPE_BASE ** (-jnp.arange(0, QK_ROPE, 2, dtype=F32) / QK_ROPE)
    ang = pos[:, None] * inv[None, :]
    cos, sin = jnp.cos(ang), jnp.sin(ang)
    ones = jnp.ones((seq, QK_NOPE), F32)
    zn = jnp.zeros((seq, QK_NOPE), F32)
    tc = _slot_lanes(jnp.concatenate([ones, cos, cos], axis=1))
    ts = _slot_lanes(jnp.concatenate([zn, -sin, sin], axis=1))
    return tc, ts, tc.T, ts.T


def _prepare_weights(norm1_g, w_in, q_a_norm_g, kv_a_norm_g, w_q_b, w_kv_b, q_norm_g, k_norm_g,
                     w_o_a, sgu_norm_g, w_s, b_s, w_o_b, w_out, norm2_g, w_ffn_in, w_ffn_out):
    L = DEPTH
    lat_end = Q_LORA + KV_LORA + QK_ROPE
    w_kr = jnp.pad(w_in[:, :, Q_LORA + KV_LORA:lat_end], ((0, 0), (0, 0), (QK_NOPE, 0)))
    w_all = jnp.concatenate([w_in[:, :, :Q_LORA + KV_LORA].astype(BF16),
                             _slot_lanes(w_kr).astype(BF16),
                             w_in[:, :, lat_end:].astype(BF16)], axis=-1)
    kv4 = w_kv_b.reshape(L, KV_LORA, N_HEADS, QK_NOPE + V_DIM)
    w_kn = _head_slots(kv4[..., :QK_NOPE].reshape(L, KV_LORA, N_HEADS * QK_NOPE), 0, QK_NOPE)
    w_v = jnp.pad(kv4[..., QK_NOPE:], ((0, 0), (0, 0), (0, 0), (0, V_ROWS - V_DIM)))
    w_v = w_v.reshape(L, KV_LORA, V_SLOTS)
    v_ones = np.zeros((V_SLOTS, TM_IN), np.float32)
    for hd in range(N_HEADS):
        v_ones[hd * V_ROWS + V_DIM, :] = 1.0
    q_scale = (QK_DIM ** -0.5) * math.log2(math.e)
    gq_slot = _slot_lanes(q_norm_g * q_scale).reshape(L, 1, LANES)
    gk_slot = _slot_lanes(k_norm_g).reshape(L, 1, LANES)
    return {
        "norm1_g": norm1_g.reshape(L, 1, D_MODEL),
        "w_all": w_all,
        "q_a_norm_g": q_a_norm_g.reshape(L, 1, Q_LORA),
        "kv_a_norm_g": kv_a_norm_g.reshape(L, 1, KV_LORA),
        "w_q_t": jnp.swapaxes(_head_slots(w_q_b, 0, QK_DIM), 1, 2).astype(BF16),
        "w_kn_b": w_kn.astype(BF16),
        "w_v_t": jnp.swapaxes(w_v, 1, 2).astype(BF16),
        "gq_col": jnp.swapaxes(gq_slot, 1, 2),
        "gq_rolled_col": jnp.swapaxes(jnp.roll(gq_slot, LANES // 2, axis=-1), 1, 2),
        "gk_slot": gk_slot,
        "gk_rolled": jnp.roll(gk_slot, LANES // 2, axis=-1),
        "v_ones": jnp.asarray(v_ones),
        "sgu_norm_g": sgu_norm_g.reshape(L, 1, D_MODEL),
        "w_s": w_s.astype(BF16),
        "b_s_full": jnp.repeat(jnp.swapaxes(b_s, 1, 2), CHUNK, axis=2),
        "w_o_b": w_o_b.astype(BF16),
        "w_o_a": w_o_a.astype(BF16),
        "w_out": w_out.astype(BF16),
        "norm2_g": norm2_g.reshape(L, 1, D_MODEL),
        "w_ffn_in": w_ffn_in.astype(BF16),
        "w_ffn_out": w_ffn_out.astype(BF16),
    }


def kernel(x_prompt, x_sample, c_prompt, c_sample, w_ada, b_ada, norm1_g, w_in, q_a_norm_g,
           kv_a_norm_g, w_q_b, w_kv_b, q_norm_g, k_norm_g, w_o_a, sgu_norm_g, w_s, b_s, w_o_b,
           w_out, norm2_g, w_ffn_in, w_ffn_out):
    w = _prepare_weights(norm1_g, w_in, q_a_norm_g, kv_a_norm_g, w_q_b, w_kv_b, q_norm_g,
                         k_norm_g, w_o_a, sgu_norm_g, w_s, b_s, w_o_b, w_out, norm2_g,
                         w_ffn_in, w_ffn_out)
    n_prompt, n_sample = c_prompt.shape[0], c_sample.shape[0]
    c16 = jnp.concatenate(
        [c_prompt, c_sample, jnp.zeros((MOD_ROWS - n_prompt - n_sample, D_MODEL), F32)], axis=0)
    mod = _modulation(c16, w_ada, b_ada)
    mod5 = mod.reshape(DEPTH, MOD_ROWS, N_MOD, 1, D_MODEL)

    groups = []
    for x, row0 in ((x_prompt, 0), (x_sample, n_prompt)):
        batch, seq, _ = x.shape
        groups.append([x.reshape(batch * seq, D_MODEL), row0, batch, seq, _rope_tables(seq)])

    for layer in range(DEPTH):
        for grp in groups:
            x2d, row0, batch, seq, tabs = grp
            qt, k, vt, sga, pb = _mixer_in(layer, x2d, mod5, row0, seq, w, tabs)
            o = _attention(qt, k, vt, batch, seq)
            grp[0] = _mixer_out(layer, x2d, o, sga, pb, mod5, row0, seq, w)

    return tuple(grp[0].reshape(grp[2], grp[3], D_MODEL) for grp in groups)
```

```python
import math

import jax
import jax.numpy as jnp
import numpy as np
from jax import lax
from jax.experimental import pallas as pl
from jax.experimental.pallas import tpu as pltpu

D_MODEL = 1024
DEPTH = 4
N_HEADS = 8
QK_NOPE = 64
QK_ROPE = 32
QK_DIM = QK_NOPE + QK_ROPE
V_DIM = 64
Q_LORA = 384
KV_LORA = 256
ROPE_BASE = 10000.0
CHUNK = 128
SGU_GROUPS = 8
D_FF = 2816
N_MOD = 6
EPS = 1e-6

LANES = 128
SLOTS = N_HEADS * LANES
V_ROWS = 80
V_SLOTS = N_HEADS * V_ROWS
LAT_COLS = 768
MOD_ROWS = 16
VMEM_LIMIT = 56 * 1024 * 1024

TM_IN = 256
TM_OUT = 256
TQ = 1024
TK = TM_IN
TS = 256
STEPS_PER_BODY = 4
MOD_TN = 512

F32 = jnp.float32
BF16 = jnp.bfloat16


def _dot(a, b):
    return jnp.dot(a, b, preferred_element_type=F32)


def _rms_scale(x, n):
    return lax.rsqrt(jnp.sum(x * x, axis=-1, keepdims=True) * (1.0 / n) + EPS)


def _gelu_tanh(x):
    c = math.sqrt(2.0 / math.pi)
    hx = 0.5 * x
    return hx + hx * jnp.tanh(x * (c + (c * 0.044715) * (x * x)))


def _sigmoid(x):
    return 1.0 / (1.0 + jnp.exp(-x))


def _mod_kernel(c_ref, w_ref, b_ref, o_ref):
    c = c_ref[...]
    a = (c * _sigmoid(c)).astype(BF16)
    o_ref[...] = _dot(a, w_ref[...].astype(BF16)) + b_ref[...]


def _modulation(c16, w_ada, b_ada):
    n_cols = N_MOD * D_MODEL
    return pl.pallas_call(
        _mod_kernel,
        out_shape=jax.ShapeDtypeStruct((DEPTH, MOD_ROWS, n_cols), F32),
        grid=(DEPTH, n_cols // MOD_TN),
        in_specs=[
            pl.BlockSpec((MOD_ROWS, D_MODEL), lambda l, j: (0, 0)),
            pl.BlockSpec((None, D_MODEL, MOD_TN), lambda l, j: (l, 0, j)),
            pl.BlockSpec((None, 1, MOD_TN), lambda l, j: (l, 0, j)),
        ],
        out_specs=pl.BlockSpec((None, MOD_ROWS, MOD_TN), lambda l, j: (l, 0, j)),
        compiler_params=pltpu.CompilerParams(
            dimension_semantics=("arbitrary", "arbitrary"),
            vmem_limit_bytes=VMEM_LIMIT),
        name="adaln_modulation",
    )(c16, w_ada, b_ada.reshape(DEPTH, 1, n_cols))


def _norm_rope(x, a, b):
    return _rms_scale(x, QK_DIM) * (x * a + pltpu.roll(x, LANES // 2, axis=1) * b)


def _mixer_in_kernel(x_ref, sh_ref, sc_ref, n1g_ref, wall_ref, gqa_ref, gkva_ref,
                     wqt_ref, wknb_ref, wvt_ref, gqc_ref, gqrc_ref, gk_ref, gkr_ref, vone_ref,
                     tc_ref, ts_ref, tct_ref, tst_ref, gsgu_ref, ws_ref, bs_ref, wob_ref,
                     qt_ref, k_ref, vt_ref, sga_ref, pb_ref, prod_scr):
    tm = x_ref.shape[0]
    x = x_ref[...]
    h = (x * _rms_scale(x, D_MODEL)) * (n1g_ref[...] * (1.0 + sc_ref[...])) + sh_ref[...]
    hb = h.astype(BF16)

    wbig_ref = wall_ref.at[:, LAT_COLS:]
    lat = _dot(hb, wall_ref[:, :LAT_COLS])
    q_lat = lat[:, :Q_LORA]
    kv_lat = lat[:, Q_LORA:Q_LORA + KV_LORA]
    kr_slot = lat[:, Q_LORA + KV_LORA:]
    qn = (q_lat * _rms_scale(q_lat, Q_LORA) * gqa_ref[...]).astype(BF16)
    kvn = (kv_lat * _rms_scale(kv_lat, KV_LORA) * gkva_ref[...]).astype(BF16)
    nt = (((1,), (1,)), ((), ()))
    qt_all = lax.dot_general(wqt_ref[...], qn, nt, preferred_element_type=F32)
    kn_all = _dot(kvn, wknb_ref[...])
    vt = lax.dot_general(wvt_ref[...], kvn, nt, preferred_element_type=F32)
    vt_ref[...] = (vt + vone_ref[...]).astype(BF16)

    tc, ts = tc_ref[...], ts_ref[...]
    ka, kb = tc * gk_ref[...], ts * gkr_ref[...]
    qat, qbt = tct_ref[...] * gqc_ref[...], tst_ref[...] * gqrc_ref[...]

    def qk_heads(heads):
        for hd in heads:
            sl = slice(hd * LANES, (hd + 1) * LANES)
            xt = qt_all[sl, :]
            rt = lax.rsqrt(jnp.sum(xt * xt, axis=0, keepdims=True) * (1.0 / QK_DIM) + EPS)
            swapped = jnp.concatenate([xt[LANES // 2:], xt[:LANES // 2]], axis=0)
            qt_ref[sl, :] = (rt * (xt * qat + swapped * qbt)).astype(BF16)
            k_ref[:, sl] = _norm_rope(kn_all[:, sl] + kr_slot, ka, kb).astype(BF16)

    zu = _dot(hb, wbig_ref[:, 0:D_MODEL])
    qk_heads(range(0, 2))
    zv = _dot(hb, wbig_ref[:, D_MODEL:2 * D_MODEL])
    qk_heads(range(2, 4))
    u = _gelu_tanh(zu)
    za = _dot(hb, wbig_ref[:, 2 * D_MODEL:3 * D_MODEL])
    qk_heads(range(4, 6))
    gv = _gelu_tanh(zv)
    vn = (gv * _rms_scale(gv, D_MODEL) * gsgu_ref[...]).astype(BF16)
    zb = _dot(hb, wbig_ref[:, 3 * D_MODEL:4 * D_MODEL])
    qk_heads(range(6, 8))

    for c in range(0, tm // CHUNK, 2):
        rows0 = slice(c * CHUNK, (c + 1) * CHUNK)
        rows1 = slice((c + 1) * CHUNK, (c + 2) * CHUNK)
        for g in range(SGU_GROUPS):
            cols = slice(g * CHUNK, (g + 1) * CHUNK)
            pair = jnp.concatenate([vn[rows0, cols], vn[rows1, cols]], axis=1)
            mixed = _dot(ws_ref[g], pair)
            bias = bs_ref[:, cols]
            prod_scr[rows0, cols] = (u[rows0, cols] * (mixed[:, :CHUNK] + bias)).astype(BF16)
            prod_scr[rows1, cols] = (u[rows1, cols] * (mixed[:, CHUNK:] + bias)).astype(BF16)
    sga_ref[...] = _sigmoid(za)
    out_b = _dot(prod_scr[...], wob_ref[...])
    pb_ref[...] = _sigmoid(zb) * out_b


def _const_spec(shape, layer=None):
    if layer is None:
        return pl.BlockSpec(shape, lambda i: (0,) * len(shape), pipeline_mode=pl.Buffered(1))
    return pl.BlockSpec((None,) + shape, lambda i: (layer,) + (0,) * len(shape),
                        pipeline_mode=pl.Buffered(1))


def _mod_spec(layer, which, row0, tiles_per_seq):
    return pl.BlockSpec((None, None, None, 1, D_MODEL),
                        lambda i: (layer, row0 + i // tiles_per_seq, which, 0, 0))


def _mixer_in(layer, x2d, mod5, row0, seq, w, tabs):
    n_tok = x2d.shape[0]
    tm = TM_IN
    tps = seq // tm
    tile = lambda cols: pl.BlockSpec((tm, cols), lambda i: (i, 0))
    tab_spec = pl.BlockSpec((tm, LANES), lambda i: (i % tps, 0))
    tabt_spec = pl.BlockSpec((LANES, tm), lambda i: (0, i % tps))
    in_specs = [
        tile(D_MODEL),
        _mod_spec(layer, 0, row0, tps),
        _mod_spec(layer, 1, row0, tps),
        _const_spec((1, D_MODEL), layer),
        _const_spec((D_MODEL, LAT_COLS + 4 * D_MODEL), layer),
        _const_spec((1, Q_LORA), layer),
        _const_spec((1, KV_LORA), layer),
        _const_spec((SLOTS, Q_LORA), layer),
        _const_spec((KV_LORA, SLOTS), layer),
        _const_spec((V_SLOTS, KV_LORA), layer),
        _const_spec((LANES, 1), layer),
        _const_spec((LANES, 1), layer),
        _const_spec((1, LANES), layer),
        _const_spec((1, LANES), layer),
        _const_spec((V_SLOTS, tm)),
        tab_spec, tab_spec, tabt_spec, tabt_spec,
        _const_spec((1, D_MODEL), layer),
        _const_spec((SGU_GROUPS, CHUNK, CHUNK), layer),
        _const_spec((CHUNK, D_MODEL), layer),
        _const_spec((D_MODEL, D_MODEL), layer),
    ]
    out_shape = (
        jax.ShapeDtypeStruct((SLOTS, n_tok), BF16),
        jax.ShapeDtypeStruct((n_tok, SLOTS), BF16),
        jax.ShapeDtypeStruct((n_tok // tm, V_SLOTS, tm), BF16),
        jax.ShapeDtypeStruct((n_tok, D_MODEL), F32),
        jax.ShapeDtypeStruct((n_tok, D_MODEL), F32),
    )
    out_specs = (pl.BlockSpec((SLOTS, tm), lambda i: (0, i)), tile(SLOTS),
                 pl.BlockSpec((None, V_SLOTS, tm), lambda i: (i, 0, 0)),
                 tile(D_MODEL), tile(D_MODEL))
    return pl.pallas_call(
        _mixer_in_kernel,
        out_shape=out_shape,
        grid=(n_tok // tm,),
        in_specs=in_specs,
        out_specs=out_specs,
        scratch_shapes=[pltpu.VMEM((tm, D_MODEL), BF16)],
        compiler_params=pltpu.CompilerParams(
            dimension_semantics=("arbitrary",), vmem_limit_bytes=VMEM_LIMIT),
        name="mixer_in",
    )(x2d, mod5, mod5, w["norm1_g"], w["w_all"], w["q_a_norm_g"], w["kv_a_norm_g"],
      w["w_q_t"], w["w_kn_b"], w["w_v_t"], w["gq_col"], w["gq_rolled_col"], w["gk_slot"],
      w["gk_rolled"], w["v_ones"], tabs[0], tabs[1], tabs[2], tabs[3], w["sgu_norm_g"], w["w_s"],
      w["b_s_full"], w["w_o_b"])


def _attn_kernel(qt_ref, k_ref, vt_ref, o_ref, m_scr, acc_scr, *parity_bufs):
    tq = qt_ref.shape[1]
    n_chunks = vt_ref.shape[0]
    m_scr[...] = jnp.full(m_scr.shape, -jnp.inf, F32)
    acc_scr[...] = jnp.zeros(acc_scr.shape, F32)
    st_scr, cm_scr = parity_bufs[0::2], parity_bufs[1::2]

    items = [(hh, sub) for hh in range(2) for sub in range(tq // TS)]

    def scores(item, j, par):
        hh, sub = item
        sl = slice(hh * LANES, (hh + 1) * LANES)
        cs = slice(sub * TS, (sub + 1) * TS)
        kc = k_ref[pl.ds(pl.multiple_of(j * TK, TK), TK), sl]
        st = _dot(kc, qt_ref[sl, cs])
        st_scr[par][hh, :, cs] = st
        cm_scr[par][hh, :, cs] = jnp.max(st, axis=0, keepdims=True)

    def probs(item, par):
        hh, sub = item
        cs = slice(sub * TS, (sub + 1) * TS)
        m_old = m_scr[hh, :, cs]
        m_new = jnp.maximum(m_old, cm_scr[par][hh, :, cs])
        m_scr[hh, :, cs] = m_new
        pt = jnp.exp2(st_scr[par][hh, :, cs] - m_new).astype(BF16)
        return pt, jnp.exp2(m_old - m_new)

    def accumulate(item, j, pt, alpha):
        hh, sub = item
        rows = slice(hh * V_ROWS, (hh + 1) * V_ROWS)
        cs = slice(sub * TS, (sub + 1) * TS)
        acc_scr[hh, :, cs] = alpha * acc_scr[hh, :, cs] + _dot(vt_ref[j, rows, :], pt)

    def step(j, par, do_scores=True):
        for item in items:
            pt, alpha = probs(item, par)
            if do_scores:
                scores(item, j + 1, 1 - par)
            accumulate(item, j, pt, alpha)

    for item in items:
        scores(item, 0, 0)

    spb = STEPS_PER_BODY if n_chunks > 2 * STEPS_PER_BODY else 2

    def body(i, carry):
        for s in range(spb):
            step(spb * i + s, s % 2)
        return carry

    full = (n_chunks - 1) // spb
    lax.fori_loop(0, full, body, 0)
    for j in range(spb * full, n_chunks - 1):
        step(j, j % 2)
    step(n_chunks - 1, (n_chunks - 1) % 2, do_scores=False)
    outs = []
    for hh in range(2):
        acc = acc_scr[hh]
        outs.append(acc[:V_DIM, :] * (1.0 / acc[V_DIM:V_DIM + 1, :]))
    o_ref[...] = jnp.concatenate(outs, axis=0).T.astype(BF16)


def _attention(qt, k, vt, batch, seq):
    n_tok = k.shape[0]
    tq = seq if seq <= 2 * TQ else TQ
    n_qt = seq // tq
    pair = 2 * LANES
    return pl.pallas_call(
        _attn_kernel,
        out_shape=jax.ShapeDtypeStruct((n_tok, N_HEADS * V_DIM), BF16),
        grid=(batch, N_HEADS // 2, n_qt),
        in_specs=[
            pl.BlockSpec((pair, tq), lambda b, hp, i: (hp, b * n_qt + i)),
            pl.BlockSpec((seq, pair), lambda b, hp, i: (b, hp)),
            pl.BlockSpec((seq // TK, 2 * V_ROWS, TK), lambda b, hp, i: (b, hp, 0)),
        ],
        out_specs=pl.BlockSpec((tq, LANES), lambda b, hp, i: (b * n_qt + i, hp)),
        scratch_shapes=[
            pltpu.VMEM((2, 1, tq), F32),
            pltpu.VMEM((2, V_ROWS, tq), F32),
        ] + 2 * [
            pltpu.VMEM((2, TK, tq), F32),
            pltpu.VMEM((2, 1, tq), F32),
        ],
        compiler_params=pltpu.CompilerParams(
            dimension_semantics=("arbitrary", "arbitrary", "arbitrary"),
            vmem_limit_bytes=VMEM_LIMIT),
        name="attention",
    )(qt, k, vt)


def _mixer_out_kernel(x_ref, o_ref, sga_ref, pb_ref, g1_ref, sh2_ref, sc2_ref, g2_ref,
                      woa_ref, wout_ref, n2g_ref, wfi_ref, wfo_ref, y_ref):
    out_a = _dot(o_ref[...], woa_ref[...])
    merged = (sga_ref[...] * out_a + pb_ref[...]).astype(BF16)
    x1 = x_ref[...] + g1_ref[...] * _dot(merged, wout_ref[...])
    h2 = (x1 * _rms_scale(x1, D_MODEL)) * (n2g_ref[...] * (1.0 + sc2_ref[...])) + sh2_ref[...]
    h2 = h2.astype(BF16)
    up = _dot(h2, wfi_ref[:, :D_FF])
    gate = _dot(h2, wfi_ref[:, D_FF:])
    act = (gate * _sigmoid(gate) * up).astype(BF16)
    y_ref[...] = x1 + g2_ref[...] * _dot(act, wfo_ref[...])


def _mixer_out(layer, x2d, o, sga, pb, mod5, row0, seq, w):
    n_tok = x2d.shape[0]
    tm = TM_OUT
    tps = seq // tm
    tile = lambda cols: pl.BlockSpec((tm, cols), lambda i: (i, 0))
    in_specs = [
        tile(D_MODEL), tile(N_HEADS * V_DIM), tile(D_MODEL), tile(D_MODEL),
        _mod_spec(layer, 2, row0, tps),
        _mod_spec(layer, 3, row0, tps),
        _mod_spec(layer, 4, row0, tps),
        _mod_spec(layer, 5, row0, tps),
        _const_spec((N_HEADS * V_DIM, D_MODEL), layer),
        _const_spec((D_MODEL, D_MODEL), layer),
        _const_spec((1, D_MODEL), layer),
        _const_spec((D_MODEL, 2 * D_FF), layer),
        _const_spec((D_FF, D_MODEL), layer),
    ]
    return pl.pallas_call(
        _mixer_out_kernel,
        out_shape=jax.ShapeDtypeStruct((n_tok, D_MODEL), F32),
        grid=(n_tok // tm,),
        in_specs=in_specs,
        out_specs=tile(D_MODEL),
        compiler_params=pltpu.CompilerParams(
            dimension_semantics=("arbitrary",), vmem_limit_bytes=VMEM_LIMIT),
        name="mixer_out",
    )(x2d, o, sga, pb, mod5, mod5, mod5, mod5,
      w["w_o_a"], w["w_out"], w["norm2_g"], w["w_ffn_in"], w["w_ffn_out"])


def _slot_source():
    half = QK_ROPE // 2
    src = np.full((LANES,), QK_DIM, np.int32)
    src[0:48] = np.arange(0, 48)
    src[48:48 + half] = np.arange(QK_NOPE, QK_NOPE + half)
    src[64:80] = np.arange(48, QK_NOPE)
    src[LANES - half:] = np.arange(QK_NOPE + half, QK_DIM)
    return src


def _slot_lanes(a):
    a = jnp.pad(a, [(0, 0)] * (a.ndim - 1) + [(0, 1)])
    return jnp.take(a, _slot_source(), axis=-1)


def _head_slots(w, first_dim, width):
    L, K, _ = w.shape
    w = w.reshape(L, K, N_HEADS, width)
    w = jnp.pad(w, ((0, 0), (0, 0), (0, 0), (first_dim, QK_DIM - first_dim - width)))
    return _slot_lanes(w).reshape(L, K, SLOTS)


def _rope_tables(seq):
    pos = jnp.arange(seq, dtype=F32)
    inv = ROPE_BASE ** (-jnp.arange(0, QK_ROPE, 2, dtype=F32) / QK_ROPE)
    ang = pos[:, None] * inv[None, :]
    cos, sin = jnp.cos(ang), jnp.sin(ang)
    ones = jnp.ones((seq, QK_NOPE), F32)
    zn = jnp.zeros((seq, QK_NOPE), F32)
    tc = _slot_lanes(jnp.concatenate([ones, cos, cos], axis=1))
    ts = _slot_lanes(jnp.concatenate([zn, -sin, sin], axis=1))
    return tc, ts, tc.T, ts.T


def _prepare_weights(norm1_g, w_in, q_a_norm_g, kv_a_norm_g, w_q_b, w_kv_b, q_norm_g, k_norm_g,
                     w_o_a, sgu_norm_g, w_s, b_s, w_o_b, w_out, norm2_g, w_ffn_in, w_ffn_out):
    L = DEPTH
    lat_end = Q_LORA + KV_LORA + QK_ROPE
    w_kr = jnp.pad(w_in[:, :, Q_LORA + KV_LORA:lat_end], ((0, 0), (0, 0), (QK_NOPE, 0)))
    w_all = jnp.concatenate([w_in[:, :, :Q_LORA + KV_LORA].astype(BF16),
                             _slot_lanes(w_kr).astype(BF16),
                             w_in[:, :, lat_end:].astype(BF16)], axis=-1)
    kv4 = w_kv_b.reshape(L, KV_LORA, N_HEADS, QK_NOPE + V_DIM)
    w_kn = _head_slots(kv4[..., :QK_NOPE].reshape(L, KV_LORA, N_HEADS * QK_NOPE), 0, QK_NOPE)
    w_v = jnp.pad(kv4[..., QK_NOPE:], ((0, 0), (0, 0), (0, 0), (0, V_ROWS - V_DIM)))
    w_v = w_v.reshape(L, KV_LORA, V_SLOTS)
    v_ones = np.zeros((V_SLOTS, TM_IN), np.float32)
    for hd in range(N_HEADS):
        v_ones[hd * V_ROWS + V_DIM, :] = 1.0
    q_scale = (QK_DIM ** -0.5) * math.log2(math.e)
    gq_slot = _slot_lanes(q_norm_g * q_scale).reshape(L, 1, LANES)
    gk_slot = _slot_lanes(k_norm_g).reshape(L, 1, LANES)
    return {
        "norm1_g": norm1_g.reshape(L, 1, D_MODEL),
        "w_all": w_all,
        "q_a_norm_g": q_a_norm_g.reshape(L, 1, Q_LORA),
        "kv_a_norm_g": kv_a_norm_g.reshape(L, 1, KV_LORA),
        "w_q_t": jnp.swapaxes(_head_slots(w_q_b, 0, QK_DIM), 1, 2).astype(BF16),
        "w_kn_b": w_kn.astype(BF16),
        "w_v_t": jnp.swapaxes(w_v, 1, 2).astype(BF16),
        "gq_col": jnp.swapaxes(gq_slot, 1, 2),
        "gq_rolled_col": jnp.swapaxes(jnp.roll(gq_slot, LANES // 2, axis=-1), 1, 2),
        "gk_slot": gk_slot,
        "gk_rolled": jnp.roll(gk_slot, LANES // 2, axis=-1),
        "v_ones": jnp.asarray(v_ones),
        "sgu_norm_g": sgu_norm_g.reshape(L, 1, D_MODEL),
        "w_s": w_s.astype(BF16),
        "b_s_full": jnp.repeat(jnp.swapaxes(b_s, 1, 2), CHUNK, axis=2),
        "w_o_b": w_o_b.astype(BF16),
        "w_o_a": w_o_a.astype(BF16),
        "w_out": w_out.astype(BF16),
        "norm2_g": norm2_g.reshape(L, 1, D_MODEL),
        "w_ffn_in": w_ffn_in.astype(BF16),
        "w_ffn_out": w_ffn_out.astype(BF16),
    }


def kernel(x_prompt, x_sample, c_prompt, c_sample, w_ada, b_ada, norm1_g, w_in, q_a_norm_g,
           kv_a_norm_g, w_q_b, w_kv_b, q_norm_g, k_norm_g, w_o_a, sgu_norm_g, w_s, b_s, w_o_b,
           w_out, norm2_g, w_ffn_in, w_ffn_out):
    w = _prepare_weights(norm1_g, w_in, q_a_norm_g, kv_a_norm_g, w_q_b, w_kv_b, q_norm_g,
                         k_norm_g, w_o_a, sgu_norm_g, w_s, b_s, w_o_b, w_out, norm2_g,
                         w_ffn_in, w_ffn_out)
    n_prompt, n_sample = c_prompt.shape[0], c_sample.shape[0]
    c16 = jnp.concatenate(
        [c_prompt, c_sample, jnp.zeros((MOD_ROWS - n_prompt - n_sample, D_MODEL), F32)], axis=0)
    mod = _modulation(c16, w_ada, b_ada)
    mod5 = mod.reshape(DEPTH, MOD_ROWS, N_MOD, 1, D_MODEL)

    groups = []
    for x, row0 in ((x_prompt, 0), (x_sample, n_prompt)):
        batch, seq, _ = x.shape
        groups.append([x.reshape(batch * seq, D_MODEL), row0, batch, seq, _rope_tables(seq)])

    for layer in range(DEPTH):
        for grp in groups:
            x2d, row0, batch, seq, tabs = grp
            qt, k, vt, sga, pb = _mixer_in(layer, x2d, mod5, row0, seq, w, tabs)
            o = _attention(qt, k, vt, batch, seq)
            grp[0] = _mixer_out(layer, x2d, o, sga, pb, mod5, row0, seq, w)

    return tuple(grp[0].reshape(grp[2], grp[3], D_MODEL) for grp in groups)
```

```python
import math

import jax
import jax.numpy as jnp
import numpy as np
from jax import lax
from jax.experimental import pallas as pl
from jax.experimental.pallas import tpu as pltpu

D_MODEL = 1024
DEPTH = 4
N_HEADS = 8
QK_NOPE = 64
QK_ROPE = 32
QK_DIM = QK_NOPE + QK_ROPE
V_DIM = 64
Q_LORA = 384
KV_LORA = 256
ROPE_BASE = 10000.0
CHUNK = 128
SGU_GROUPS = 8
D_FF = 2816
N_MOD = 6
EPS = 1e-6

LANES = 128
SLOTS = N_HEADS * LANES
V_ROWS = 80
V_SLOTS = N_HEADS * V_ROWS
LAT_COLS = 768
MOD_ROWS = 16
VMEM_LIMIT = 56 * 1024 * 1024

TM_IN = 256
TM_OUT = 256
TQ = 2048
TK = TM_IN
TS = 256
STEPS_PER_BODY = 4
MOD_TN = 512

F32 = jnp.float32
BF16 = jnp.bfloat16


def _dot(a, b):
    return jnp.dot(a, b, preferred_element_type=F32)


def _rms_scale(x, n):
    return lax.rsqrt(jnp.sum(x * x, axis=-1, keepdims=True) * (1.0 / n) + EPS)


def _gelu_tanh(x):
    c = math.sqrt(2.0 / math.pi)
    hx = 0.5 * x
    return hx + hx * jnp.tanh(x * (c + (c * 0.044715) * (x * x)))


def _sigmoid(x):
    return 1.0 / (1.0 + jnp.exp(-x))


def _mod_kernel(c_ref, w_ref, b_ref, o_ref):
    c = c_ref[...]
    a = (c * _sigmoid(c)).astype(BF16)
    o_ref[...] = _dot(a, w_ref[...].astype(BF16)) + b_ref[...]


def _modulation(c16, w_ada, b_ada):
    n_cols = N_MOD * D_MODEL
    return pl.pallas_call(
        _mod_kernel,
        out_shape=jax.ShapeDtypeStruct((DEPTH, MOD_ROWS, n_cols), F32),
        grid=(DEPTH, n_cols // MOD_TN),
        in_specs=[
            pl.BlockSpec((MOD_ROWS, D_MODEL), lambda l, j: (0, 0)),
            pl.BlockSpec((None, D_MODEL, MOD_TN), lambda l, j: (l, 0, j)),
            pl.BlockSpec((None, 1, MOD_TN), lambda l, j: (l, 0, j)),
        ],
        out_specs=pl.BlockSpec((None, MOD_ROWS, MOD_TN), lambda l, j: (l, 0, j)),
        compiler_params=pltpu.CompilerParams(
            dimension_semantics=("arbitrary", "arbitrary"),
            vmem_limit_bytes=VMEM_LIMIT),
        name="adaln_modulation",
    )(c16, w_ada, b_ada.reshape(DEPTH, 1, n_cols))


def _norm_rope(x, a, b):
    return _rms_scale(x, QK_DIM) * (x * a + pltpu.roll(x, LANES // 2, axis=1) * b)


def _mixer_in_kernel(x_ref, sh_ref, sc_ref, n1g_ref, wall_ref, gqa_ref, gkva_ref,
                     wqt_ref, wknb_ref, wvt_ref, gqc_ref, gqrc_ref, gk_ref, gkr_ref, vone_ref,
                     tc_ref, ts_ref, tct_ref, tst_ref, gsgu_ref, ws_ref, bs_ref, wob_ref,
                     qt_ref, k_ref, vt_ref, sga_ref, pb_ref, prod_scr):
    tm = x_ref.shape[0]
    x = x_ref[...]
    h = (x * _rms_scale(x, D_MODEL)) * (n1g_ref[...] * (1.0 + sc_ref[...])) + sh_ref[...]
    hb = h.astype(BF16)

    wbig_ref = wall_ref.at[:, LAT_COLS:]
    lat = _dot(hb, wall_ref[:, :LAT_COLS])
    q_lat = lat[:, :Q_LORA]
    kv_lat = lat[:, Q_LORA:Q_LORA + KV_LORA]
    kr_slot = lat[:, Q_LORA + KV_LORA:]
    qn = (q_lat * _rms_scale(q_lat, Q_LORA) * gqa_ref[...]).astype(BF16)
    kvn = (kv_lat * _rms_scale(kv_lat, KV_LORA) * gkva_ref[...]).astype(BF16)
    nt = (((1,), (1,)), ((), ()))
    qt_all = lax.dot_general(wqt_ref[...], qn, nt, preferred_element_type=F32)
    kn_all = _dot(kvn, wknb_ref[...])
    vt = lax.dot_general(wvt_ref[...], kvn, nt, preferred_element_type=F32)
    vt_ref[...] = (vt + vone_ref[...]).astype(BF16)

    tc, ts = tc_ref[...], ts_ref[...]
    ka, kb = tc * gk_ref[...], ts * gkr_ref[...]
    qat, qbt = tct_ref[...] * gqc_ref[...], tst_ref[...] * gqrc_ref[...]

    def qk_heads(heads):
        for hd in heads:
            sl = slice(hd * LANES, (hd + 1) * LANES)
            xt = qt_all[sl, :]
            rt = lax.rsqrt(jnp.sum(xt * xt, axis=0, keepdims=True) * (1.0 / QK_DIM) + EPS)
            swapped = jnp.concatenate([xt[LANES // 2:], xt[:LANES // 2]], axis=0)
            qt_ref[sl, :] = (rt * (xt * qat + swapped * qbt)).astype(BF16)
            k_ref[:, sl] = _norm_rope(kn_all[:, sl] + kr_slot, ka, kb).astype(BF16)

    zu = _dot(hb, wbig_ref[:, 0:D_MODEL])
    qk_heads(range(0, 2))
    zv = _dot(hb, wbig_ref[:, D_MODEL:2 * D_MODEL])
    qk_heads(range(2, 4))
    u = _gelu_tanh(zu)
    za = _dot(hb, wbig_ref[:, 2 * D_MODEL:3 * D_MODEL])
    qk_heads(range(4, 6))
    gv = _gelu_tanh(zv)
    vn = (gv * _rms_scale(gv, D_MODEL) * gsgu_ref[...]).astype(BF16)
    zb = _dot(hb, wbig_ref[:, 3 * D_MODEL:4 * D_MODEL])
    qk_heads(range(6, 8))

    for c in range(0, tm // CHUNK, 2):
        rows0 = slice(c * CHUNK, (c + 1) * CHUNK)
        rows1 = slice((c + 1) * CHUNK, (c + 2) * CHUNK)
        for g in range(SGU_GROUPS):
            cols = slice(g * CHUNK, (g + 1) * CHUNK)
            pair = jnp.concatenate([vn[rows0, cols], vn[rows1, cols]], axis=1)
            mixed = _dot(ws_ref[g], pair)
            bias = bs_ref[:, cols]
            prod_scr[rows0, cols] = (u[rows0, cols] * (mixed[:, :CHUNK] + bias)).astype(BF16)
            prod_scr[rows1, cols] = (u[rows1, cols] * (mixed[:, CHUNK:] + bias)).astype(BF16)
    sga_ref[...] = _sigmoid(za)
    out_b = _dot(prod_scr[...], wob_ref[...])
    pb_ref[...] = _sigmoid(zb) * out_b


def _const_spec(shape, layer=None):
    if layer is None:
        return pl.BlockSpec(shape, lambda i: (0,) * len(shape), pipeline_mode=pl.Buffered(1))
    return pl.BlockSpec((None,) + shape, lambda i: (layer,) + (0,) * len(shape),
                        pipeline_mode=pl.Buffered(1))


def _mod_spec(layer, which, row0, tiles_per_seq):
    return pl.BlockSpec((None, None, None, 1, D_MODEL),
                        lambda i: (layer, row0 + i // tiles_per_seq, which, 0, 0))


def _mixer_in(layer, x2d, mod5, row0, seq, w, tabs):
    n_tok = x2d.shape[0]
    tm = TM_IN
    tps = seq // tm
    tile = lambda cols: pl.BlockSpec((tm, cols), lambda i: (i, 0))
    tab_spec = pl.BlockSpec((tm, LANES), lambda i: (i % tps, 0))
    tabt_spec = pl.BlockSpec((LANES, tm), lambda i: (0, i % tps))
    in_specs = [
        tile(D_MODEL),
        _mod_spec(layer, 0, row0, tps),
        _mod_spec(layer, 1, row0, tps),
        _const_spec((1, D_MODEL), layer),
        _const_spec((D_MODEL, LAT_COLS + 4 * D_MODEL), layer),
        _const_spec((1, Q_LORA), layer),
        _const_spec((1, KV_LORA), layer),
        _const_spec((SLOTS, Q_LORA), layer),
        _const_spec((KV_LORA, SLOTS), layer),
        _const_spec((V_SLOTS, KV_LORA), layer),
        _const_spec((LANES, 1), layer),
        _const_spec((LANES, 1), layer),
        _const_spec((1, LANES), layer),
        _const_spec((1, LANES), layer),
        _const_spec((V_SLOTS, tm)),
        tab_spec, tab_spec, tabt_spec, tabt_spec,
        _const_spec((1, D_MODEL), layer),
        _const_spec((SGU_GROUPS, CHUNK, CHUNK), layer),
        _const_spec((CHUNK, D_MODEL), layer),
        _const_spec((D_MODEL, D_MODEL), layer),
    ]
    out_shape = (
        jax.ShapeDtypeStruct((SLOTS, n_tok), BF16),
        jax.ShapeDtypeStruct((n_tok, SLOTS), BF16),
        jax.ShapeDtypeStruct((n_tok // tm, V_SLOTS, tm), BF16),
        jax.ShapeDtypeStruct((n_tok, D_MODEL), F32),
        jax.ShapeDtypeStruct((n_tok, D_MODEL), F32),
    )
    out_specs = (pl.BlockSpec((SLOTS, tm), lambda i: (0, i)), tile(SLOTS),
                 pl.BlockSpec((None, V_SLOTS, tm), lambda i: (i, 0, 0)),
                 tile(D_MODEL), tile(D_MODEL))
    return pl.pallas_call(
        _mixer_in_kernel,
        out_shape=out_shape,
        grid=(n_tok // tm,),
        in_specs=in_specs,
        out_specs=out_specs,
        scratch_shapes=[pltpu.VMEM((tm, D_MODEL), BF16)],
        compiler_params=pltpu.CompilerParams(
            dimension_semantics=("arbitrary",), vmem_limit_bytes=VMEM_LIMIT),
        name="mixer_in",
    )(x2d, mod5, mod5, w["norm1_g"], w["w_all"], w["q_a_norm_g"], w["kv_a_norm_g"],
      w["w_q_t"], w["w_kn_b"], w["w_v_t"], w["gq_col"], w["gq_rolled_col"], w["gk_slot"],
      w["gk_rolled"], w["v_ones"], tabs[0], tabs[1], tabs[2], tabs[3], w["sgu_norm_g"], w["w_s"],
      w["b_s_full"], w["w_o_b"])


def _attn_kernel(qt_ref, k_ref, vt_ref, o_ref, m_scr, acc_scr, *parity_bufs):
    tq = qt_ref.shape[1]
    n_chunks = vt_ref.shape[0]
    m_scr[...] = jnp.full(m_scr.shape, -jnp.inf, F32)
    acc_scr[...] = jnp.zeros(acc_scr.shape, F32)
    st_scr, cm_scr = parity_bufs[0::2], parity_bufs[1::2]

    items = [(hh, sub) for hh in range(2) for sub in range(tq // TS)]

    def scores(item, j, par):
        hh, sub = item
        sl = slice(hh * LANES, (hh + 1) * LANES)
        cs = slice(sub * TS, (sub + 1) * TS)
        kc = k_ref[pl.ds(pl.multiple_of(j * TK, TK), TK), sl]
        st = _dot(kc, qt_ref[sl, cs])
        st_scr[par][hh, :, cs] = st
        cm_scr[par][hh, :, cs] = jnp.max(st, axis=0, keepdims=True)

    def probs(item, par):
        hh, sub = item
        cs = slice(sub * TS, (sub + 1) * TS)
        m_old = m_scr[hh, :, cs]
        m_new = jnp.maximum(m_old, cm_scr[par][hh, :, cs])
        m_scr[hh, :, cs] = m_new
        pt = jnp.exp2(st_scr[par][hh, :, cs] - m_new).astype(BF16)
        return pt, jnp.exp2(m_old - m_new)

    def accumulate(item, j, pt, alpha):
        hh, sub = item
        rows = slice(hh * V_ROWS, (hh + 1) * V_ROWS)
        cs = slice(sub * TS, (sub + 1) * TS)
        acc_scr[hh, :, cs] = alpha * acc_scr[hh, :, cs] + _dot(vt_ref[j, rows, :], pt)

    def step(j, par, do_scores=True):
        for item in items:
            pt, alpha = probs(item, par)
            if do_scores:
                scores(item, j + 1, 1 - par)
            accumulate(item, j, pt, alpha)

    for item in items:
        scores(item, 0, 0)

    spb = STEPS_PER_BODY if n_chunks > 2 * STEPS_PER_BODY else 2

    def body(i, carry):
        for s in range(spb):
            step(spb * i + s, s % 2)
        return carry

    full = (n_chunks - 1) // spb
    lax.fori_loop(0, full, body, 0)
    for j in range(spb * full, n_chunks - 1):
        step(j, j % 2)
    step(n_chunks - 1, (n_chunks - 1) % 2, do_scores=False)
    outs = []
    for hh in range(2):
        acc = acc_scr[hh]
        outs.append(acc[:V_DIM, :] * (1.0 / acc[V_DIM:V_DIM + 1, :]))
    o_ref[...] = jnp.concatenate(outs, axis=0).T.astype(BF16)


def _attention(qt, k, vt, batch, seq):
    n_tok = k.shape[0]
    tq = min(seq, TQ)
    n_qt = seq // tq
    pair = 2 * LANES
    return pl.pallas_call(
        _attn_kernel,
        out_shape=jax.ShapeDtypeStruct((n_tok, N_HEADS * V_DIM), BF16),
        grid=(batch, N_HEADS // 2, n_qt),
        in_specs=[
            pl.BlockSpec((pair, tq), lambda b, hp, i: (hp, b * n_qt + i)),
            pl.BlockSpec((seq, pair), lambda b, hp, i: (b, hp)),
            pl.BlockSpec((seq // TK, 2 * V_ROWS, TK), lambda b, hp, i: (b, hp, 0)),
        ],
        out_specs=pl.BlockSpec((tq, LANES), lambda b, hp, i: (b * n_qt + i, hp)),
        scratch_shapes=[
            pltpu.VMEM((2, 1, tq), F32),
            pltpu.VMEM((2, V_ROWS, tq), F32),
        ] + 2 * [
            pltpu.VMEM((2, TK, tq), F32),
            pltpu.VMEM((2, 1, tq), F32),
        ],
        compiler_params=pltpu.CompilerParams(
            dimension_semantics=("arbitrary", "arbitrary", "arbitrary"),
            vmem_limit_bytes=VMEM_LIMIT),
        name="attention",
    )(qt, k, vt)


def _mixer_out_kernel(x_ref, o_ref, sga_ref, pb_ref, g1_ref, sh2_ref, sc2_ref, g2_ref,
                      woa_ref, wout_ref, n2g_ref, wfi_ref, wfo_ref, y_ref):
    out_a = _dot(o_ref[...], woa_ref[...])
    merged = (sga_ref[...] * out_a + pb_ref[...]).astype(BF16)
    x1 = x_ref[...] + g1_ref[...] * _dot(merged, wout_ref[...])
    h2 = (x1 * _rms_scale(x1, D_MODEL)) * (n2g_ref[...] * (1.0 + sc2_ref[...])) + sh2_ref[...]
    h2 = h2.astype(BF16)
    up = _dot(h2, wfi_ref[:, :D_FF])
    gate = _dot(h2, wfi_ref[:, D_FF:])
    act = (gate * _sigmoid(gate) * up).astype(BF16)
    y_ref[...] = x1 + g2_ref[...] * _dot(act, wfo_ref[...])


def _mixer_out(layer, x2d, o, sga, pb, mod5, row0, seq, w):
    n_tok = x2d.shape[0]
    tm = TM_OUT
    tps = seq // tm
    tile = lambda cols: pl.BlockSpec((tm, cols), lambda i: (i, 0))
    in_specs = [
        tile(D_MODEL), tile(N_HEADS * V_DIM), tile(D_MODEL), tile(D_MODEL),
        _mod_spec(layer, 2, row0, tps),
        _mod_spec(layer, 3, row0, tps),
        _mod_spec(layer, 4, row0, tps),
        _mod_spec(layer, 5, row0, tps),
        _const_spec((N_HEADS * V_DIM, D_MODEL), layer),
        _const_spec((D_MODEL, D_MODEL), layer),
        _const_spec((1, D_MODEL), layer),
        _const_spec((D_MODEL, 2 * D_FF), layer),
        _const_spec((D_FF, D_MODEL), layer),
    ]
    return pl.pallas_call(
        _mixer_out_kernel,
        out_shape=jax.ShapeDtypeStruct((n_tok, D_MODEL), F32),
        grid=(n_tok // tm,),
        in_specs=in_specs,
        out_specs=tile(D_MODEL),
        compiler_params=pltpu.CompilerParams(
            dimension_semantics=("arbitrary",), vmem_limit_bytes=VMEM_LIMIT),
        name="mixer_out",
    )(x2d, o, sga, pb, mod5, mod5, mod5, mod5,
      w["w_o_a"], w["w_out"], w["norm2_g"], w["w_ffn_in"], w["w_ffn_out"])


def _slot_source():
    half = QK_ROPE // 2
    src = np.full((LANES,), QK_DIM, np.int32)
    src[0:48] = np.arange(0, 48)
    src[48:48 + half] = np.arange(QK_NOPE, QK_NOPE + half)
    src[64:80] = np.arange(48, QK_NOPE)
    src[LANES - half:] = np.arange(QK_NOPE + half, QK_DIM)
    return src


def _slot_lanes(a):
    a = jnp.pad(a, [(0, 0)] * (a.ndim - 1) + [(0, 1)])
    return jnp.take(a, _slot_source(), axis=-1)


def _head_slots(w, first_dim, width):
    L, K, _ = w.shape
    w = w.reshape(L, K, N_HEADS, width)
    w = jnp.pad(w, ((0, 0), (0, 0), (0, 0), (first_dim, QK_DIM - first_dim - width)))
    return _slot_lanes(w).reshape(L, K, SLOTS)


def _rope_tables(seq):
    pos = jnp.arange(seq, dtype=F32)
    inv = ROPE_BASE ** (-jnp.arange(0, QK_ROPE, 2, dtype=F32) / QK_ROPE)
    ang = pos[:, None] * inv[None, :]
    cos, sin = jnp.cos(ang), jnp.sin(ang)
    ones = jnp.ones((seq, QK_NOPE), F32)
    zn = jnp.zeros((seq, QK_NOPE), F32)
    tc = _slot_lanes(jnp.concatenate([ones, cos, cos], axis=1))
    ts = _slot_lanes(jnp.concatenate([zn, -sin, sin], axis=1))
    return tc, ts, tc.T, ts.T


def _prepare_weights(norm1_g, w_in, q_a_norm_g, kv_a_norm_g, w_q_b, w_kv_b, q_norm_g, k_norm_g,
                     w_o_a, sgu_norm_g, w_s, b_s, w_o_b, w_out, norm2_g, w_ffn_in, w_ffn_out):
    L = DEPTH
    lat_end = Q_LORA + KV_LORA + QK_ROPE
    w_kr = jnp.pad(w_in[:, :, Q_LORA + KV_LORA:lat_end], ((0, 0), (0, 0), (QK_NOPE, 0)))
    w_all = jnp.concatenate([w_in[:, :, :Q_LORA + KV_LORA].astype(BF16),
                             _slot_lanes(w_kr).astype(BF16),
                             w_in[:, :, lat_end:].astype(BF16)], axis=-1)
    kv4 = w_kv_b.reshape(L, KV_LORA, N_HEADS, QK_NOPE + V_DIM)
    w_kn = _head_slots(kv4[..., :QK_NOPE].reshape(L, KV_LORA, N_HEADS * QK_NOPE), 0, QK_NOPE)
    w_v = jnp.pad(kv4[..., QK_NOPE:], ((0, 0), (0, 0), (0, 0), (0, V_ROWS - V_DIM)))
    w_v = w_v.reshape(L, KV_LORA, V_SLOTS)
    v_ones = np.zeros((V_SLOTS, TM_IN), np.float32)
    for hd in range(N_HEADS):
        v_ones[hd * V_ROWS + V_DIM, :] = 1.0
    q_scale = (QK_DIM ** -0.5) * math.log2(math.e)
    gq_slot = _slot_lanes(q_norm_g * q_scale).reshape(L, 1, LANES)
    gk_slot = _slot_lanes(k_norm_g).reshape(L, 1, LANES)
    return {
        "norm1_g": norm1_g.reshape(L, 1, D_MODEL),
        "w_all": w_all,
        "q_a_norm_g": q_a_norm_g.reshape(L, 1, Q_LORA),
        "kv_a_norm_g": kv_a_norm_g.reshape(L, 1, KV_LORA),
        "w_q_t": jnp.swapaxes(_head_slots(w_q_b, 0, QK_DIM), 1, 2).astype(BF16),
        "w_kn_b": w_kn.astype(BF16),
        "w_v_t": jnp.swapaxes(w_v, 1, 2).astype(BF16),
        "gq_col": jnp.swapaxes(gq_slot, 1, 2),
        "gq_rolled_col": jnp.swapaxes(jnp.roll(gq_slot, LANES // 2, axis=-1), 1, 2),
        "gk_slot": gk_slot,
        "gk_rolled": jnp.roll(gk_slot, LANES // 2, axis=-1),
        "v_ones": jnp.asarray(v_ones),
        "sgu_norm_g": sgu_norm_g.reshape(L, 1, D_MODEL),
        "w_s": w_s.astype(BF16),
        "b_s_full": jnp.repeat(jnp.swapaxes(b_s, 1, 2), CHUNK, axis=2),
        "w_o_b": w_o_b.astype(BF16),
        "w_o_a": w_o_a.astype(BF16),
        "w_out": w_out.astype(BF16),
        "norm2_g": norm2_g.reshape(L, 1, D_MODEL),
        "w_ffn_in": w_ffn_in.astype(BF16),
        "w_ffn_out": w_ffn_out.astype(BF16),
    }


def kernel(x_prompt, x_sample, c_prompt, c_sample, w_ada, b_ada, norm1_g, w_in, q_a_norm_g,
           kv_a_norm_g, w_q_b, w_kv_b, q_norm_g, k_norm_g, w_o_a, sgu_norm_g, w_s, b_s, w_o_b,
           w_out, norm2_g, w_ffn_in, w_ffn_out):
    w = _prepare_weights(norm1_g, w_in, q_a_norm_g, kv_a_norm_g, w_q_b, w_kv_b, q_norm_g,
                         k_norm_g, w_o_a, sgu_norm_g, w_s, b_s, w_o_b, w_out, norm2_g,
                         w_ffn_in, w_ffn_out)
    n_prompt, n_sample = c_prompt.shape[0], c_sample.shape[0]
    c16 = jnp.concatenate(
        [c_prompt, c_sample, jnp.zeros((MOD_ROWS - n_prompt - n_sample, D_MODEL), F32)], axis=0)
    mod = _modulation(c16, w_ada, b_ada)
    mod5 = mod.reshape(DEPTH, MOD_ROWS, N_MOD, 1, D_MODEL)

    groups = []
    for x, row0 in ((x_prompt, 0), (x_sample, n_prompt)):
        batch, seq, _ = x.shape
        groups.append([x.reshape(batch * seq, D_MODEL), row0, batch, seq, _rope_tables(seq)])

    for layer in range(DEPTH):
        for grp in groups:
            x2d, row0, batch, seq, tabs = grp
            qt, k, vt, sga, pb = _mixer_in(layer, x2d, mod5, row0, seq, w, tabs)
            o = _attention(qt, k, vt, batch, seq)
            grp[0] = _mixer_out(layer, x2d, o, sga, pb, mod5, row0, seq, w)

    return tuple(grp[0].reshape(grp[2], grp[3], D_MODEL) for grp in groups)
```

```python
import math

import jax
import jax.numpy as jnp
import numpy as np
from jax import lax
from jax.experimental import pallas as pl
from jax.experimental.pallas import tpu as pltpu

D_MODEL = 1024
DEPTH = 4
N_HEADS = 8
QK_NOPE = 64
QK_ROPE = 32
QK_DIM = QK_NOPE + QK_ROPE
V_DIM = 64
Q_LORA = 384
KV_LORA = 256
ROPE_BASE = 10000.0
CHUNK = 128
SGU_GROUPS = 8
D_FF = 2816
N_MOD = 6
EPS = 1e-6

LANES = 128
SLOTS = N_HEADS * LANES
V_ROWS = 80
V_SLOTS = N_HEADS * V_ROWS
LAT_COLS = 768
MOD_ROWS = 16
VMEM_LIMIT = 56 * 1024 * 1024

TM_IN = 256
TM_OUT = 256
TQ = 4096
TK = TM_IN
TS = 256
STEPS_PER_BODY = 4
MOD_TN = 512

F32 = jnp.float32
BF16 = jnp.bfloat16


def _dot(a, b):
    return jnp.dot(a, b, preferred_element_type=F32)


def _rms_scale(x, n):
    return lax.rsqrt(jnp.sum(x * x, axis=-1, keepdims=True) * (1.0 / n) + EPS)


def _gelu_tanh(x):
    c = math.sqrt(2.0 / math.pi)
    hx = 0.5 * x
    return hx + hx * jnp.tanh(x * (c + (c * 0.044715) * (x * x)))


def _sigmoid(x):
    return 1.0 / (1.0 + jnp.exp(-x))


def _mod_kernel(c_ref, w_ref, b_ref, o_ref):
    c = c_ref[...]
    a = (c * _sigmoid(c)).astype(BF16)
    o_ref[...] = _dot(a, w_ref[...].astype(BF16)) + b_ref[...]


def _modulation(c16, w_ada, b_ada):
    n_cols = N_MOD * D_MODEL
    return pl.pallas_call(
        _mod_kernel,
        out_shape=jax.ShapeDtypeStruct((DEPTH, MOD_ROWS, n_cols), F32),
        grid=(DEPTH, n_cols // MOD_TN),
        in_specs=[
            pl.BlockSpec((MOD_ROWS, D_MODEL), lambda l, j: (0, 0)),
            pl.BlockSpec((None, D_MODEL, MOD_TN), lambda l, j: (l, 0, j)),
            pl.BlockSpec((None, 1, MOD_TN), lambda l, j: (l, 0, j)),
        ],
        out_specs=pl.BlockSpec((None, MOD_ROWS, MOD_TN), lambda l, j: (l, 0, j)),
        compiler_params=pltpu.CompilerParams(
            dimension_semantics=("arbitrary", "arbitrary"),
            vmem_limit_bytes=VMEM_LIMIT),
        name="adaln_modulation",
    )(c16, w_ada, b_ada.reshape(DEPTH, 1, n_cols))


def _norm_rope(x, a, b):
    return _rms_scale(x, QK_DIM) * (x * a + pltpu.roll(x, LANES // 2, axis=1) * b)


def _mixer_in_kernel(x_ref, sh_ref, sc_ref, n1g_ref, wall_ref, gqa_ref, gkva_ref,
                     wqt_ref, wknb_ref, wvt_ref, gqc_ref, gqrc_ref, gk_ref, gkr_ref, vone_ref,
                     tc_ref, ts_ref, tct_ref, tst_ref, gsgu_ref, ws_ref, bs_ref, wob_ref,
                     qt_ref, k_ref, vt_ref, sga_ref, pb_ref, prod_scr):
    tm = x_ref.shape[0]
    x = x_ref[...]
    h = (x * _rms_scale(x, D_MODEL)) * (n1g_ref[...] * (1.0 + sc_ref[...])) + sh_ref[...]
    hb = h.astype(BF16)

    wbig_ref = wall_ref.at[:, LAT_COLS:]
    lat = _dot(hb, wall_ref[:, :LAT_COLS])
    q_lat = lat[:, :Q_LORA]
    kv_lat = lat[:, Q_LORA:Q_LORA + KV_LORA]
    kr_slot = lat[:, Q_LORA + KV_LORA:]
    qn = (q_lat * _rms_scale(q_lat, Q_LORA) * gqa_ref[...]).astype(BF16)
    kvn = (kv_lat * _rms_scale(kv_lat, KV_LORA) * gkva_ref[...]).astype(BF16)
    nt = (((1,), (1,)), ((), ()))
    qt_all = lax.dot_general(wqt_ref[...], qn, nt, preferred_element_type=F32)
    kn_all = _dot(kvn, wknb_ref[...])
    vt = lax.dot_general(wvt_ref[...], kvn, nt, preferred_element_type=F32)
    vt_ref[...] = (vt + vone_ref[...]).astype(BF16)

    tc, ts = tc_ref[...], ts_ref[...]
    ka, kb = tc * gk_ref[...], ts * gkr_ref[...]
    qat, qbt = tct_ref[...] * gqc_ref[...], tst_ref[...] * gqrc_ref[...]

    def qk_heads(heads):
        for hd in heads:
            sl = slice(hd * LANES, (hd + 1) * LANES)
            xt = qt_all[sl, :]
            rt = lax.rsqrt(jnp.sum(xt * xt, axis=0, keepdims=True) * (1.0 / QK_DIM) + EPS)
            swapped = jnp.concatenate([xt[LANES // 2:], xt[:LANES // 2]], axis=0)
            qt_ref[sl, :] = (rt * (xt * qat + swapped * qbt)).astype(BF16)
            k_ref[:, sl] = _norm_rope(kn_all[:, sl] + kr_slot, ka, kb).astype(BF16)

    zu = _dot(hb, wbig_ref[:, 0:D_MODEL])
    qk_heads(range(0, 2))
    zv = _dot(hb, wbig_ref[:, D_MODEL:2 * D_MODEL])
    qk_heads(range(2, 4))
    u = _gelu_tanh(zu)
    za = _dot(hb, wbig_ref[:, 2 * D_MODEL:3 * D_MODEL])
    qk_heads(range(4, 6))
    gv = _gelu_tanh(zv)
    vn = (gv * _rms_scale(gv, D_MODEL) * gsgu_ref[...]).astype(BF16)
    zb = _dot(hb, wbig_ref[:, 3 * D_MODEL:4 * D_MODEL])
    qk_heads(range(6, 8))

    for c in range(0, tm // CHUNK, 2):
        rows0 = slice(c * CHUNK, (c + 1) * CHUNK)
        rows1 = slice((c + 1) * CHUNK, (c + 2) * CHUNK)
        for g in range(SGU_GROUPS):
            cols = slice(g * CHUNK, (g + 1) * CHUNK)
            pair = jnp.concatenate([vn[rows0, cols], vn[rows1, cols]], axis=1)
            mixed = _dot(ws_ref[g], pair)
            bias = bs_ref[:, cols]
            prod_scr[rows0, cols] = (u[rows0, cols] * (mixed[:, :CHUNK] + bias)).astype(BF16)
            prod_scr[rows1, cols] = (u[rows1, cols] * (mixed[:, CHUNK:] + bias)).astype(BF16)
    sga_ref[...] = _sigmoid(za)
    out_b = _dot(prod_scr[...], wob_ref[...])
    pb_ref[...] = _sigmoid(zb) * out_b


def _const_spec(shape, layer=None):
    if layer is None:
        return pl.BlockSpec(shape, lambda i: (0,) * len(shape), pipeline_mode=pl.Buffered(1))
    return pl.BlockSpec((None,) + shape, lambda i: (layer,) + (0,) * len(shape),
                        pipeline_mode=pl.Buffered(1))


def _mod_spec(layer, which, row0, tiles_per_seq):
    return pl.BlockSpec((None, None, None, 1, D_MODEL),
                        lambda i: (layer, row0 + i // tiles_per_seq, which, 0, 0))


def _mixer_in(layer, x2d, mod5, row0, seq, w, tabs):
    n_tok = x2d.shape[0]
    tm = TM_IN
    tps = seq // tm
    tile = lambda cols: pl.BlockSpec((tm, cols), lambda i: (i, 0))
    tab_spec = pl.BlockSpec((tm, LANES), lambda i: (i % tps, 0))
    tabt_spec = pl.BlockSpec((LANES, tm), lambda i: (0, i % tps))
    in_specs = [
        tile(D_MODEL),
        _mod_spec(layer, 0, row0, tps),
        _mod_spec(layer, 1, row0, tps),
        _const_spec((1, D_MODEL), layer),
        _const_spec((D_MODEL, LAT_COLS + 4 * D_MODEL), layer),
        _const_spec((1, Q_LORA), layer),
        _const_spec((1, KV_LORA), layer),
        _const_spec((SLOTS, Q_LORA), layer),
        _const_spec((KV_LORA, SLOTS), layer),
        _const_spec((V_SLOTS, KV_LORA), layer),
        _const_spec((LANES, 1), layer),
        _const_spec((LANES, 1), layer),
        _const_spec((1, LANES), layer),
        _const_spec((1, LANES), layer),
        _const_spec((V_SLOTS, tm)),
        tab_spec, tab_spec, tabt_spec, tabt_spec,
        _const_spec((1, D_MODEL), layer),
        _const_spec((SGU_GROUPS, CHUNK, CHUNK), layer),
        _const_spec((CHUNK, D_MODEL), layer),
        _const_spec((D_MODEL, D_MODEL), layer),
    ]
    out_shape = (
        jax.ShapeDtypeStruct((SLOTS, n_tok), BF16),
        jax.ShapeDtypeStruct((n_tok, SLOTS), BF16),
        jax.ShapeDtypeStruct((n_tok // tm, V_SLOTS, tm), BF16),
        jax.ShapeDtypeStruct((n_tok, D_MODEL), F32),
        jax.ShapeDtypeStruct((n_tok, D_MODEL), F32),
    )
    out_specs = (pl.BlockSpec((SLOTS, tm), lambda i: (0, i)), tile(SLOTS),
                 pl.BlockSpec((None, V_SLOTS, tm), lambda i: (i, 0, 0)),
                 tile(D_MODEL), tile(D_MODEL))
    return pl.pallas_call(
        _mixer_in_kernel,
        out_shape=out_shape,
        grid=(n_tok // tm,),
        in_specs=in_specs,
        out_specs=out_specs,
        scratch_shapes=[pltpu.VMEM((tm, D_MODEL), BF16)],
        compiler_params=pltpu.CompilerParams(
            dimension_semantics=("arbitrary",), vmem_limit_bytes=VMEM_LIMIT),
        name="mixer_in",
    )(x2d, mod5, mod5, w["norm1_g"], w["w_all"], w["q_a_norm_g"], w["kv_a_norm_g"],
      w["w_q_t"], w["w_kn_b"], w["w_v_t"], w["gq_col"], w["gq_rolled_col"], w["gk_slot"],
      w["gk_rolled"], w["v_ones"], tabs[0], tabs[1], tabs[2], tabs[3], w["sgu_norm_g"], w["w_s"],
      w["b_s_full"], w["w_o_b"])


def _attn_kernel(qt_ref, k_ref, vt_ref, o_ref, m_scr, acc_scr, *parity_bufs):
    tq = qt_ref.shape[1]
    n_chunks = vt_ref.shape[0]
    m_scr[...] = jnp.full(m_scr.shape, -jnp.inf, F32)
    acc_scr[...] = jnp.zeros(acc_scr.shape, F32)
    st_scr, cm_scr = parity_bufs[0::2], parity_bufs[1::2]

    items = [(hh, sub) for hh in range(2) for sub in range(tq // TS)]

    def scores(item, j, par):
        hh, sub = item
        sl = slice(hh * LANES, (hh + 1) * LANES)
        cs = slice(sub * TS, (sub + 1) * TS)
        kc = k_ref[pl.ds(pl.multiple_of(j * TK, TK), TK), sl]
        st = _dot(kc, qt_ref[sl, cs])
        st_scr[par][hh, :, cs] = st
        cm_scr[par][hh, :, cs] = jnp.max(st, axis=0, keepdims=True)

    def probs(item, par):
        hh, sub = item
        cs = slice(sub * TS, (sub + 1) * TS)
        m_old = m_scr[hh, :, cs]
        m_new = jnp.maximum(m_old, cm_scr[par][hh, :, cs])
        m_scr[hh, :, cs] = m_new
        pt = jnp.exp2(st_scr[par][hh, :, cs] - m_new).astype(BF16)
        return pt, jnp.exp2(m_old - m_new)

    def accumulate(item, j, pt, alpha):
        hh, sub = item
        rows = slice(hh * V_ROWS, (hh + 1) * V_ROWS)
        cs = slice(sub * TS, (sub + 1) * TS)
        acc_scr[hh, :, cs] = alpha * acc_scr[hh, :, cs] + _dot(vt_ref[j, rows, :], pt)

    def step(j, par, do_scores=True):
        for item in items:
            pt, alpha = probs(item, par)
            if do_scores:
                scores(item, j + 1, 1 - par)
            accumulate(item, j, pt, alpha)

    for item in items:
        scores(item, 0, 0)

    spb = STEPS_PER_BODY if n_chunks > 2 * STEPS_PER_BODY else 2

    def body(i, carry):
        for s in range(spb):
            step(spb * i + s, s % 2)
        return carry

    full = (n_chunks - 1) // spb
    lax.fori_loop(0, full, body, 0)
    for j in range(spb * full, n_chunks - 1):
        step(j, j % 2)
    step(n_chunks - 1, (n_chunks - 1) % 2, do_scores=False)
    outs = []
    for hh in range(2):
        acc = acc_scr[hh]
        outs.append(acc[:V_DIM, :] * (1.0 / acc[V_DIM:V_DIM + 1, :]))
    o_ref[...] = jnp.concatenate(outs, axis=0).T.astype(BF16)


def _attention(qt, k, vt, batch, seq):
    n_tok = k.shape[0]
    tq = min(seq, TQ)
    n_qt = seq // tq
    pair = 2 * LANES
    return pl.pallas_call(
        _attn_kernel,
        out_shape=jax.ShapeDtypeStruct((n_tok, N_HEADS * V_DIM), BF16),
        grid=(batch, N_HEADS // 2, n_qt),
        in_specs=[
            pl.BlockSpec((pair, tq), lambda b, hp, i: (hp, b * n_qt + i)),
            pl.BlockSpec((seq, pair), lambda b, hp, i: (b, hp)),
            pl.BlockSpec((seq // TK, 2 * V_ROWS, TK), lambda b, hp, i: (b, hp, 0)),
        ],
        out_specs=pl.BlockSpec((tq, LANES), lambda b, hp, i: (b * n_qt + i, hp)),
        scratch_shapes=[
            pltpu.VMEM((2, 1, tq), F32),
            pltpu.VMEM((2, V_ROWS, tq), F32),
        ] + 2 * [
            pltpu.VMEM((2, TK, tq), F32),
            pltpu.VMEM((2, 1, tq), F32),
        ],
        compiler_params=pltpu.CompilerParams(
            dimension_semantics=("arbitrary", "arbitrary", "arbitrary"),
            vmem_limit_bytes=VMEM_LIMIT),
        name="attention",
    )(qt, k, vt)


def _mixer_out_kernel(x_ref, o_ref, sga_ref, pb_ref, g1_ref, sh2_ref, sc2_ref, g2_ref,
                      woa_ref, wout_ref, n2g_ref, wfi_ref, wfo_ref, y_ref):
    out_a = _dot(o_ref[...], woa_ref[...])
    merged = (sga_ref[...] * out_a + pb_ref[...]).astype(BF16)
    x1 = x_ref[...] + g1_ref[...] * _dot(merged, wout_ref[...])
    h2 = (x1 * _rms_scale(x1, D_MODEL)) * (n2g_ref[...] * (1.0 + sc2_ref[...])) + sh2_ref[...]
    h2 = h2.astype(BF16)
    up = _dot(h2, wfi_ref[:, :D_FF])
    gate = _dot(h2, wfi_ref[:, D_FF:])
    act = (gate * _sigmoid(gate) * up).astype(BF16)
    y_ref[...] = x1 + g2_ref[...] * _dot(act, wfo_ref[...])


def _mixer_out(layer, x2d, o, sga, pb, mod5, row0, seq, w):
    n_tok = x2d.shape[0]
    tm = TM_OUT
    tps = seq // tm
    tile = lambda cols: pl.BlockSpec((tm, cols), lambda i: (i, 0))
    in_specs = [
        tile(D_MODEL), tile(N_HEADS * V_DIM), tile(D_MODEL), tile(D_MODEL),
        _mod_spec(layer, 2, row0, tps),
        _mod_spec(layer, 3, row0, tps),
        _mod_spec(layer, 4, row0, tps),
        _mod_spec(layer, 5, row0, tps),
        _const_spec((N_HEADS * V_DIM, D_MODEL), layer),
        _const_spec((D_MODEL, D_MODEL), layer),
        _const_spec((1, D_MODEL), layer),
        _const_spec((D_MODEL, 2 * D_FF), layer),
        _const_spec((D_FF, D_MODEL), layer),
    ]
    return pl.pallas_call(
        _mixer_out_kernel,
        out_shape=jax.ShapeDtypeStruct((n_tok, D_MODEL), F32),
        grid=(n_tok // tm,),
        in_specs=in_specs,
        out_specs=tile(D_MODEL),
        compiler_params=pltpu.CompilerParams(
            dimension_semantics=("arbitrary",), vmem_limit_bytes=VMEM_LIMIT),
        name="mixer_out",
    )(x2d, o, sga, pb, mod5, mod5, mod5, mod5,
      w["w_o_a"], w["w_out"], w["norm2_g"], w["w_ffn_in"], w["w_ffn_out"])


def _slot_source():
    half = QK_ROPE // 2
    src = np.full((LANES,), QK_DIM, np.int32)
    src[0:48] = np.arange(0, 48)
    src[48:48 + half] = np.arange(QK_NOPE, QK_NOPE + half)
    src[64:80] = np.arange(48, QK_NOPE)
    src[LANES - half:] = np.arange(QK_NOPE + half, QK_DIM)
    return src


def _slot_lanes(a):
    a = jnp.pad(a, [(0, 0)] * (a.ndim - 1) + [(0, 1)])
    return jnp.take(a, _slot_source(), axis=-1)


def _head_slots(w, first_dim, width):
    L, K, _ = w.shape
    w = w.reshape(L, K, N_HEADS, width)
    w = jnp.pad(w, ((0, 0), (0, 0), (0, 0), (first_dim, QK_DIM - first_dim - width)))
    return _slot_lanes(w).reshape(L, K, SLOTS)


def _rope_tables(seq):
    pos = jnp.arange(seq, dtype=F32)
    inv = ROPE_BASE ** (-jnp.arange(0, QK_ROPE, 2, dtype=F32) / QK_ROPE)
    ang = pos[:, None] * inv[None, :]
    cos, sin = jnp.cos(ang), jnp.sin(ang)
    ones = jnp.ones((seq, QK_NOPE), F32)
    zn = jnp.zeros((seq, QK_NOPE), F32)
    tc = _slot_lanes(jnp.concatenate([ones, cos, cos], axis=1))
    ts = _slot_lanes(jnp.concatenate([zn, -sin, sin], axis=1))
    return tc, ts, tc.T, ts.T


def _prepare_weights(norm1_g, w_in, q_a_norm_g, kv_a_norm_g, w_q_b, w_kv_b, q_norm_g, k_norm_g,
                     w_o_a, sgu_norm_g, w_s, b_s, w_o_b, w_out, norm2_g, w_ffn_in, w_ffn_out):
    L = DEPTH
    lat_end = Q_LORA + KV_LORA + QK_ROPE
    w_kr = jnp.pad(w_in[:, :, Q_LORA + KV_LORA:lat_end], ((0, 0), (0, 0), (QK_NOPE, 0)))
    w_all = jnp.concatenate([w_in[:, :, :Q_LORA + KV_LORA].astype(BF16),
                             _slot_lanes(w_kr).astype(BF16),
                             w_in[:, :, lat_end:].astype(BF16)], axis=-1)
    kv4 = w_kv_b.reshape(L, KV_LORA, N_HEADS, QK_NOPE + V_DIM)
    w_kn = _head_slots(kv4[..., :QK_NOPE].reshape(L, KV_LORA, N_HEADS * QK_NOPE), 0, QK_NOPE)
    w_v = jnp.pad(kv4[..., QK_NOPE:], ((0, 0), (0, 0), (0, 0), (0, V_ROWS - V_DIM)))
    w_v = w_v.reshape(L, KV_LORA, V_SLOTS)
    v_ones = np.zeros((V_SLOTS, TM_IN), np.float32)
    for hd in range(N_HEADS):
        v_ones[hd * V_ROWS + V_DIM, :] = 1.0
    q_scale = (QK_DIM ** -0.5) * math.log2(math.e)
    gq_slot = _slot_lanes(q_norm_g * q_scale).reshape(L, 1, LANES)
    gk_slot = _slot_lanes(k_norm_g).reshape(L, 1, LANES)
    return {
        "norm1_g": norm1_g.reshape(L, 1, D_MODEL),
        "w_all": w_all,
        "q_a_norm_g": q_a_norm_g.reshape(L, 1, Q_LORA),
        "kv_a_norm_g": kv_a_norm_g.reshape(L, 1, KV_LORA),
        "w_q_t": jnp.swapaxes(_head_slots(w_q_b, 0, QK_DIM), 1, 2).astype(BF16),
        "w_kn_b": w_kn.astype(BF16),
        "w_v_t": jnp.swapaxes(w_v, 1, 2).astype(BF16),
        "gq_col": jnp.swapaxes(gq_slot, 1, 2),
        "gq_rolled_col": jnp.swapaxes(jnp.roll(gq_slot, LANES // 2, axis=-1), 1, 2),
        "gk_slot": gk_slot,
        "gk_rolled": jnp.roll(gk_slot, LANES // 2, axis=-1),
        "v_ones": jnp.asarray(v_ones),
        "sgu_norm_g": sgu_norm_g.reshape(L, 1, D_MODEL),
        "w_s": w_s.astype(BF16),
        "b_s_full": jnp.repeat(jnp.swapaxes(b_s, 1, 2), CHUNK, axis=2),
        "w_o_b": w_o_b.astype(BF16),
        "w_o_a": w_o_a.astype(BF16),
        "w_out": w_out.astype(BF16),
        "norm2_g": norm2_g.reshape(L, 1, D_MODEL),
        "w_ffn_in": w_ffn_in.astype(BF16),
        "w_ffn_out": w_ffn_out.astype(BF16),
    }


def kernel(x_prompt, x_sample, c_prompt, c_sample, w_ada, b_ada, norm1_g, w_in, q_a_norm_g,
           kv_a_norm_g, w_q_b, w_kv_b, q_norm_g, k_norm_g, w_o_a, sgu_norm_g, w_s, b_s, w_o_b,
           w_out, norm2_g, w_ffn_in, w_ffn_out):
    w = _prepare_weights(norm1_g, w_in, q_a_norm_g, kv_a_norm_g, w_q_b, w_kv_b, q_norm_g,
                         k_norm_g, w_o_a, sgu_norm_g, w_s, b_s, w_o_b, w_out, norm2_g,
                         w_ffn_in, w_ffn_out)
    n_prompt, n_sample = c_prompt.shape[0], c_sample.shape[0]
    c16 = jnp.concatenate(
        [c_prompt, c_sample, jnp.zeros((MOD_ROWS - n_prompt - n_sample, D_MODEL), F32)], axis=0)
    mod = _modulation(c16, w_ada, b_ada)
    mod5 = mod.reshape(DEPTH, MOD_ROWS, N_MOD, 1, D_MODEL)

    groups = []
    for x, row0 in ((x_prompt, 0), (x_sample, n_prompt)):
        batch, seq, _ = x.shape
        groups.append([x.reshape(batch * seq, D_MODEL), row0, batch, seq, _rope_tables(seq)])

    for layer in range(DEPTH):
        for grp in groups:
            x2d, row0, batch, seq, tabs = grp
            qt, k, vt, sga, pb = _mixer_in(layer, x2d, mod5, row0, seq, w, tabs)
            o = _attention(qt, k, vt, batch, seq)
            grp[0] = _mixer_out(layer, x2d, o, sga, pb, mod5, row0, seq, w)

    return tuple(grp[0].reshape(grp[2], grp[3], D_MODEL) for grp in groups)
```

```python
import math

import jax
import jax.numpy as jnp
import numpy as np
from jax import lax
from jax.experimental import pallas as pl
from jax.experimental.pallas import tpu as pltpu

D_MODEL = 1024
DEPTH = 4
N_HEADS = 8
QK_NOPE = 64
QK_ROPE = 32
QK_DIM = QK_NOPE + QK_ROPE
V_DIM = 64
Q_LORA = 384
KV_LORA = 256
ROPE_BASE = 10000.0
CHUNK = 128
SGU_GROUPS = 8
D_FF = 2816
N_MOD = 6
EPS = 1e-6

LANES = 128
SLOTS = N_HEADS * LANES
V_ROWS = 80
V_SLOTS = N_HEADS * V_ROWS
LAT_COLS = 768
MOD_ROWS = 16
VMEM_LIMIT = 56 * 1024 * 1024

TM_IN = 256
TM_OUT = 256
TQ = 4096
ITEMS_PER_STEP = 32
TK = TM_IN
TS = 256
STEPS_PER_BODY = 4
MOD_TN = 512

F32 = jnp.float32
BF16 = jnp.bfloat16


def _dot(a, b):
    return jnp.dot(a, b, preferred_element_type=F32)


def _rms_scale(x, n):
    return lax.rsqrt(jnp.sum(x * x, axis=-1, keepdims=True) * (1.0 / n) + EPS)


def _gelu_tanh(x):
    c = math.sqrt(2.0 / math.pi)
    hx = 0.5 * x
    return hx + hx * jnp.tanh(x * (c + (c * 0.044715) * (x * x)))


def _sigmoid(x):
    return 1.0 / (1.0 + jnp.exp(-x))


def _mod_kernel(c_ref, w_ref, b_ref, o_ref):
    c = c_ref[...]
    a = (c * _sigmoid(c)).astype(BF16)
    o_ref[...] = _dot(a, w_ref[...].astype(BF16)) + b_ref[...]


def _modulation(c16, w_ada, b_ada):
    n_cols = N_MOD * D_MODEL
    return pl.pallas_call(
        _mod_kernel,
        out_shape=jax.ShapeDtypeStruct((DEPTH, MOD_ROWS, n_cols), F32),
        grid=(DEPTH, n_cols // MOD_TN),
        in_specs=[
            pl.BlockSpec((MOD_ROWS, D_MODEL), lambda l, j: (0, 0)),
            pl.BlockSpec((None, D_MODEL, MOD_TN), lambda l, j: (l, 0, j)),
            pl.BlockSpec((None, 1, MOD_TN), lambda l, j: (l, 0, j)),
        ],
        out_specs=pl.BlockSpec((None, MOD_ROWS, MOD_TN), lambda l, j: (l, 0, j)),
        compiler_params=pltpu.CompilerParams(
            dimension_semantics=("arbitrary", "arbitrary"),
            vmem_limit_bytes=VMEM_LIMIT),
        name="adaln_modulation",
    )(c16, w_ada, b_ada.reshape(DEPTH, 1, n_cols))


def _norm_rope(x, a, b):
    return _rms_scale(x, QK_DIM) * (x * a + pltpu.roll(x, LANES // 2, axis=1) * b)


def _mixer_in_kernel(x_ref, sh_ref, sc_ref, n1g_ref, wall_ref, gqa_ref, gkva_ref,
                     wqt_ref, wknb_ref, wvt_ref, gqc_ref, gqrc_ref, gk_ref, gkr_ref, vone_ref,
                     tc_ref, ts_ref, tct_ref, tst_ref, gsgu_ref, ws_ref, bs_ref, wob_ref,
                     qt_ref, k_ref, vt_ref, sga_ref, pb_ref, prod_scr):
    tm = x_ref.shape[0]
    x = x_ref[...]
    h = (x * _rms_scale(x, D_MODEL)) * (n1g_ref[...] * (1.0 + sc_ref[...])) + sh_ref[...]
    hb = h.astype(BF16)

    wbig_ref = wall_ref.at[:, LAT_COLS:]
    lat = _dot(hb, wall_ref[:, :LAT_COLS])
    q_lat = lat[:, :Q_LORA]
    kv_lat = lat[:, Q_LORA:Q_LORA + KV_LORA]
    kr_slot = lat[:, Q_LORA + KV_LORA:]
    qn = (q_lat * _rms_scale(q_lat, Q_LORA) * gqa_ref[...]).astype(BF16)
    kvn = (kv_lat * _rms_scale(kv_lat, KV_LORA) * gkva_ref[...]).astype(BF16)
    nt = (((1,), (1,)), ((), ()))
    qt_all = lax.dot_general(wqt_ref[...], qn, nt, preferred_element_type=F32)
    kn_all = _dot(kvn, wknb_ref[...])
    vt = lax.dot_general(wvt_ref[...], kvn, nt, preferred_element_type=F32)
    vt_ref[...] = (vt + vone_ref[...]).astype(BF16)

    tc, ts = tc_ref[...], ts_ref[...]
    ka, kb = tc * gk_ref[...], ts * gkr_ref[...]
    qat, qbt = tct_ref[...] * gqc_ref[...], tst_ref[...] * gqrc_ref[...]

    def qk_heads(heads):
        for hd in heads:
            sl = slice(hd * LANES, (hd + 1) * LANES)
            xt = qt_all[sl, :]
            rt = lax.rsqrt(jnp.sum(xt * xt, axis=0, keepdims=True) * (1.0 / QK_DIM) + EPS)
            swapped = jnp.concatenate([xt[LANES // 2:], xt[:LANES // 2]], axis=0)
            qt_ref[sl, :] = (rt * (xt * qat + swapped * qbt)).astype(BF16)
            k_ref[:, sl] = _norm_rope(kn_all[:, sl] + kr_slot, ka, kb).astype(BF16)

    zu = _dot(hb, wbig_ref[:, 0:D_MODEL])
    qk_heads(range(0, 2))
    zv = _dot(hb, wbig_ref[:, D_MODEL:2 * D_MODEL])
    qk_heads(range(2, 4))
    u = _gelu_tanh(zu)
    za = _dot(hb, wbig_ref[:, 2 * D_MODEL:3 * D_MODEL])
    qk_heads(range(4, 6))
    gv = _gelu_tanh(zv)
    vn = (gv * _rms_scale(gv, D_MODEL) * gsgu_ref[...]).astype(BF16)
    zb = _dot(hb, wbig_ref[:, 3 * D_MODEL:4 * D_MODEL])
    qk_heads(range(6, 8))

    for c in range(0, tm // CHUNK, 2):
        rows0 = slice(c * CHUNK, (c + 1) * CHUNK)
        rows1 = slice((c + 1) * CHUNK, (c + 2) * CHUNK)
        for g in range(SGU_GROUPS):
            cols = slice(g * CHUNK, (g + 1) * CHUNK)
            pair = jnp.concatenate([vn[rows0, cols], vn[rows1, cols]], axis=1)
            mixed = _dot(ws_ref[g], pair)
            bias = bs_ref[:, cols]
            prod_scr[rows0, cols] = (u[rows0, cols] * (mixed[:, :CHUNK] + bias)).astype(BF16)
            prod_scr[rows1, cols] = (u[rows1, cols] * (mixed[:, CHUNK:] + bias)).astype(BF16)
    sga_ref[...] = _sigmoid(za)
    out_b = _dot(prod_scr[...], wob_ref[...])
    pb_ref[...] = _sigmoid(zb) * out_b


def _const_spec(shape, layer=None):
    if layer is None:
        return pl.BlockSpec(shape, lambda i: (0,) * len(shape), pipeline_mode=pl.Buffered(1))
    return pl.BlockSpec((None,) + shape, lambda i: (layer,) + (0,) * len(shape),
                        pipeline_mode=pl.Buffered(1))


def _mod_spec(layer, which, row0, tiles_per_seq):
    return pl.BlockSpec((None, None, None, 1, D_MODEL),
                        lambda i: (layer, row0 + i // tiles_per_seq, which, 0, 0))


def _mixer_in(layer, x2d, mod5, row0, seq, w, tabs):
    n_tok = x2d.shape[0]
    tm = TM_IN
    tps = seq // tm
    tile = lambda cols: pl.BlockSpec((tm, cols), lambda i: (i, 0))
    tab_spec = pl.BlockSpec((tm, LANES), lambda i: (i % tps, 0))
    tabt_spec = pl.BlockSpec((LANES, tm), lambda i: (0, i % tps))
    in_specs = [
        tile(D_MODEL),
        _mod_spec(layer, 0, row0, tps),
        _mod_spec(layer, 1, row0, tps),
        _const_spec((1, D_MODEL), layer),
        _const_spec((D_MODEL, LAT_COLS + 4 * D_MODEL), layer),
        _const_spec((1, Q_LORA), layer),
        _const_spec((1, KV_LORA), layer),
        _const_spec((SLOTS, Q_LORA), layer),
        _const_spec((KV_LORA, SLOTS), layer),
        _const_spec((V_SLOTS, KV_LORA), layer),
        _const_spec((LANES, 1), layer),
        _const_spec((LANES, 1), layer),
        _const_spec((1, LANES), layer),
        _const_spec((1, LANES), layer),
        _const_spec((V_SLOTS, tm)),
        tab_spec, tab_spec, tabt_spec, tabt_spec,
        _const_spec((1, D_MODEL), layer),
        _const_spec((SGU_GROUPS, CHUNK, CHUNK), layer),
        _const_spec((CHUNK, D_MODEL), layer),
        _const_spec((D_MODEL, D_MODEL), layer),
    ]
    out_shape = (
        jax.ShapeDtypeStruct((SLOTS, n_tok), BF16),
        jax.ShapeDtypeStruct((n_tok, SLOTS), BF16),
        jax.ShapeDtypeStruct((n_tok // tm, V_SLOTS, tm), BF16),
        jax.ShapeDtypeStruct((n_tok, D_MODEL), F32),
        jax.ShapeDtypeStruct((n_tok, D_MODEL), F32),
    )
    out_specs = (pl.BlockSpec((SLOTS, tm), lambda i: (0, i)), tile(SLOTS),
                 pl.BlockSpec((None, V_SLOTS, tm), lambda i: (i, 0, 0)),
                 tile(D_MODEL), tile(D_MODEL))
    return pl.pallas_call(
        _mixer_in_kernel,
        out_shape=out_shape,
        grid=(n_tok // tm,),
        in_specs=in_specs,
        out_specs=out_specs,
        scratch_shapes=[pltpu.VMEM((tm, D_MODEL), BF16)],
        compiler_params=pltpu.CompilerParams(
            dimension_semantics=("arbitrary",), vmem_limit_bytes=VMEM_LIMIT),
        name="mixer_in",
    )(x2d, mod5, mod5, w["norm1_g"], w["w_all"], w["q_a_norm_g"], w["kv_a_norm_g"],
      w["w_q_t"], w["w_kn_b"], w["w_v_t"], w["gq_col"], w["gq_rolled_col"], w["gk_slot"],
      w["gk_rolled"], w["v_ones"], tabs[0], tabs[1], tabs[2], tabs[3], w["sgu_norm_g"], w["w_s"],
      w["b_s_full"], w["w_o_b"])


def _attn_kernel(qt_ref, k_ref, vt_ref, o_ref, m_scr, acc_scr, *parity_bufs):
    n_heads, tq = qt_ref.shape[0] // LANES, qt_ref.shape[1]
    n_chunks = vt_ref.shape[0]
    m_scr[...] = jnp.full(m_scr.shape, -jnp.inf, F32)
    acc_scr[...] = jnp.zeros(acc_scr.shape, F32)
    st_scr, cm_scr = parity_bufs[0::2], parity_bufs[1::2]

    items = [(hh, sub) for hh in range(n_heads) for sub in range(tq // TS)]

    def scores(item, j, par):
        hh, sub = item
        sl = slice(hh * LANES, (hh + 1) * LANES)
        cs = slice(sub * TS, (sub + 1) * TS)
        kc = k_ref[pl.ds(pl.multiple_of(j * TK, TK), TK), sl]
        st = _dot(kc, qt_ref[sl, cs])
        st_scr[par][hh, :, cs] = st
        cm_scr[par][hh, :, cs] = jnp.max(st, axis=0, keepdims=True)

    def probs(item, par):
        hh, sub = item
        cs = slice(sub * TS, (sub + 1) * TS)
        m_old = m_scr[hh, :, cs]
        m_new = jnp.maximum(m_old, cm_scr[par][hh, :, cs])
        m_scr[hh, :, cs] = m_new
        pt = jnp.exp2(st_scr[par][hh, :, cs] - m_new).astype(BF16)
        return pt, jnp.exp2(m_old - m_new)

    def accumulate(item, j, pt, alpha):
        hh, sub = item
        rows = slice(hh * V_ROWS, (hh + 1) * V_ROWS)
        cs = slice(sub * TS, (sub + 1) * TS)
        acc_scr[hh, :, cs] = alpha * acc_scr[hh, :, cs] + _dot(vt_ref[j, rows, :], pt)

    def step(j, par, do_scores=True):
        for item in items:
            pt, alpha = probs(item, par)
            if do_scores:
                scores(item, j + 1, 1 - par)
            accumulate(item, j, pt, alpha)

    for item in items:
        scores(item, 0, 0)

    spb = STEPS_PER_BODY if n_chunks > 2 * STEPS_PER_BODY else 2

    def body(i, carry):
        for s in range(spb):
            step(spb * i + s, s % 2)
        return carry

    full = (n_chunks - 1) // spb
    lax.fori_loop(0, full, body, 0)
    for j in range(spb * full, n_chunks - 1):
        step(j, j % 2)
    step(n_chunks - 1, (n_chunks - 1) % 2, do_scores=False)
    outs = []
    for hh in range(n_heads):
        acc = acc_scr[hh]
        outs.append(acc[:V_DIM, :] * (1.0 / acc[V_DIM:V_DIM + 1, :]))
    o_ref[...] = jnp.concatenate(outs, axis=0).T.astype(BF16)


def _attention(qt, k, vt, batch, seq):
    n_tok = k.shape[0]
    tq = min(seq, TQ)
    n_qt = seq // tq
    nh = min(N_HEADS, max(2, ITEMS_PER_STEP * TS // tq))
    return pl.pallas_call(
        _attn_kernel,
        out_shape=jax.ShapeDtypeStruct((n_tok, N_HEADS * V_DIM), BF16),
        grid=(batch, N_HEADS // nh, n_qt),
        in_specs=[
            pl.BlockSpec((nh * LANES, tq), lambda b, hg, i: (hg, b * n_qt + i)),
            pl.BlockSpec((seq, nh * LANES), lambda b, hg, i: (b, hg)),
            pl.BlockSpec((seq // TK, nh * V_ROWS, TK), lambda b, hg, i: (b, hg, 0)),
        ],
        out_specs=pl.BlockSpec((tq, nh * V_DIM), lambda b, hg, i: (b * n_qt + i, hg)),
        scratch_shapes=[
            pltpu.VMEM((nh, 1, tq), F32),
            pltpu.VMEM((nh, V_ROWS, tq), F32),
        ] + 2 * [
            pltpu.VMEM((nh, TK, tq), F32),
            pltpu.VMEM((nh, 1, tq), F32),
        ],
        compiler_params=pltpu.CompilerParams(
            dimension_semantics=("arbitrary", "arbitrary", "arbitrary"),
            vmem_limit_bytes=VMEM_LIMIT),
        name="attention",
    )(qt, k, vt)


def _mixer_out_kernel(x_ref, o_ref, sga_ref, pb_ref, g1_ref, sh2_ref, sc2_ref, g2_ref,
                      woa_ref, wout_ref, n2g_ref, wfi_ref, wfo_ref, y_ref):
    out_a = _dot(o_ref[...], woa_ref[...])
    merged = (sga_ref[...] * out_a + pb_ref[...]).astype(BF16)
    x1 = x_ref[...] + g1_ref[...] * _dot(merged, wout_ref[...])
    h2 = (x1 * _rms_scale(x1, D_MODEL)) * (n2g_ref[...] * (1.0 + sc2_ref[...])) + sh2_ref[...]
    h2 = h2.astype(BF16)
    up = _dot(h2, wfi_ref[:, :D_FF])
    gate = _dot(h2, wfi_ref[:, D_FF:])
    act = (gate * _sigmoid(gate) * up).astype(BF16)
    y_ref[...] = x1 + g2_ref[...] * _dot(act, wfo_ref[...])


def _mixer_out(layer, x2d, o, sga, pb, mod5, row0, seq, w):
    n_tok = x2d.shape[0]
    tm = TM_OUT
    tps = seq // tm
    tile = lambda cols: pl.BlockSpec((tm, cols), lambda i: (i, 0))
    in_specs = [
        tile(D_MODEL), tile(N_HEADS * V_DIM), tile(D_MODEL), tile(D_MODEL),
        _mod_spec(layer, 2, row0, tps),
        _mod_spec(layer, 3, row0, tps),
        _mod_spec(layer, 4, row0, tps),
        _mod_spec(layer, 5, row0, tps),
        _const_spec((N_HEADS * V_DIM, D_MODEL), layer),
        _const_spec((D_MODEL, D_MODEL), layer),
        _const_spec((1, D_MODEL), layer),
        _const_spec((D_MODEL, 2 * D_FF), layer),
        _const_spec((D_FF, D_MODEL), layer),
    ]
    return pl.pallas_call(
        _mixer_out_kernel,
        out_shape=jax.ShapeDtypeStruct((n_tok, D_MODEL), F32),
        grid=(n_tok // tm,),
        in_specs=in_specs,
        out_specs=tile(D_MODEL),
        compiler_params=pltpu.CompilerParams(
            dimension_semantics=("arbitrary",), vmem_limit_bytes=VMEM_LIMIT),
        name="mixer_out",
    )(x2d, o, sga, pb, mod5, mod5, mod5, mod5,
      w["w_o_a"], w["w_out"], w["norm2_g"], w["w_ffn_in"], w["w_ffn_out"])


def _slot_source():
    half = QK_ROPE // 2
    src = np.full((LANES,), QK_DIM, np.int32)
    src[0:48] = np.arange(0, 48)
    src[48:48 + half] = np.arange(QK_NOPE, QK_NOPE + half)
    src[64:80] = np.arange(48, QK_NOPE)
    src[LANES - half:] = np.arange(QK_NOPE + half, QK_DIM)
    return src


def _slot_lanes(a):
    a = jnp.pad(a, [(0, 0)] * (a.ndim - 1) + [(0, 1)])
    return jnp.take(a, _slot_source(), axis=-1)


def _head_slots(w, first_dim, width):
    L, K, _ = w.shape
    w = w.reshape(L, K, N_HEADS, width)
    w = jnp.pad(w, ((0, 0), (0, 0), (0, 0), (first_dim, QK_DIM - first_dim - width)))
    return _slot_lanes(w).reshape(L, K, SLOTS)


def _rope_tables(seq):
    pos = jnp.arange(seq, dtype=F32)
    inv = ROPE_BASE ** (-jnp.arange(0, QK_ROPE, 2, dtype=F32) / QK_ROPE)
    ang = pos[:, None] * inv[None, :]
    cos, sin = jnp.cos(ang), jnp.sin(ang)
    ones = jnp.ones((seq, QK_NOPE), F32)
    zn = jnp.zeros((seq, QK_NOPE), F32)
    tc = _slot_lanes(jnp.concatenate([ones, cos, cos], axis=1))
    ts = _slot_lanes(jnp.concatenate([zn, -sin, sin], axis=1))
    return tc, ts, tc.T, ts.T


def _prepare_weights(norm1_g, w_in, q_a_norm_g, kv_a_norm_g, w_q_b, w_kv_b, q_norm_g, k_norm_g,
                     w_o_a, sgu_norm_g, w_s, b_s, w_o_b, w_out, norm2_g, w_ffn_in, w_ffn_out):
    L = DEPTH
    lat_end = Q_LORA + KV_LORA + QK_ROPE
    w_kr = jnp.pad(w_in[:, :, Q_LORA + KV_LORA:lat_end], ((0, 0), (0, 0), (QK_NOPE, 0)))
    w_all = jnp.concatenate([w_in[:, :, :Q_LORA + KV_LORA].astype(BF16),
                             _slot_lanes(w_kr).astype(BF16),
                             w_in[:, :, lat_end:].astype(BF16)], axis=-1)
    kv4 = w_kv_b.reshape(L, KV_LORA, N_HEADS, QK_NOPE + V_DIM)
    w_kn = _head_slots(kv4[..., :QK_NOPE].reshape(L, KV_LORA, N_HEADS * QK_NOPE), 0, QK_NOPE)
    w_v = jnp.pad(kv4[..., QK_NOPE:], ((0, 0), (0, 0), (0, 0), (0, V_ROWS - V_DIM)))
    w_v = w_v.reshape(L, KV_LORA, V_SLOTS)
    v_ones = np.zeros((V_SLOTS, TM_IN), np.float32)
    for hd in range(N_HEADS):
        v_ones[hd * V_ROWS + V_DIM, :] = 1.0
    q_scale = (QK_DIM ** -0.5) * math.log2(math.e)
    gq_slot = _slot_lanes(q_norm_g * q_scale).reshape(L, 1, LANES)
    gk_slot = _slot_lanes(k_norm_g).reshape(L, 1, LANES)
    return {
        "norm1_g": norm1_g.reshape(L, 1, D_MODEL),
        "w_all": w_all,
        "q_a_norm_g": q_a_norm_g.reshape(L, 1, Q_LORA),
        "kv_a_norm_g": kv_a_norm_g.reshape(L, 1, KV_LORA),
        "w_q_t": jnp.swapaxes(_head_slots(w_q_b, 0, QK_DIM), 1, 2).astype(BF16),
        "w_kn_b": w_kn.astype(BF16),
        "w_v_t": jnp.swapaxes(w_v, 1, 2).astype(BF16),
        "gq_col": jnp.swapaxes(gq_slot, 1, 2),
        "gq_rolled_col": jnp.swapaxes(jnp.roll(gq_slot, LANES // 2, axis=-1), 1, 2),
        "gk_slot": gk_slot,
        "gk_rolled": jnp.roll(gk_slot, LANES // 2, axis=-1),
        "v_ones": jnp.asarray(v_ones),
        "sgu_norm_g": sgu_norm_g.reshape(L, 1, D_MODEL),
        "w_s": w_s.astype(BF16),
        "b_s_full": jnp.repeat(jnp.swapaxes(b_s, 1, 2), CHUNK, axis=2),
        "w_o_b": w_o_b.astype(BF16),
        "w_o_a": w_o_a.astype(BF16),
        "w_out": w_out.astype(BF16),
        "norm2_g": norm2_g.reshape(L, 1, D_MODEL),
        "w_ffn_in": w_ffn_in.astype(BF16),
        "w_ffn_out": w_ffn_out.astype(BF16),
    }


def kernel(x_prompt, x_sample, c_prompt, c_sample, w_ada, b_ada, norm1_g, w_in, q_a_norm_g,
           kv_a_norm_g, w_q_b, w_kv_b, q_norm_g, k_norm_g, w_o_a, sgu_norm_g, w_s, b_s, w_o_b,
           w_out, norm2_g, w_ffn_in, w_ffn_out):
    w = _prepare_weights(norm1_g, w_in, q_a_norm_g, kv_a_norm_g, w_q_b, w_kv_b, q_norm_g,
                         k_norm_g, w_o_a, sgu_norm_g, w_s, b_s, w_o_b, w_out, norm2_g,
                         w_ffn_in, w_ffn_out)
    n_prompt, n_sample = c_prompt.shape[0], c_sample.shape[0]
    c16 = jnp.concatenate(
        [c_prompt, c_sample, jnp.zeros((MOD_ROWS - n_prompt - n_sample, D_MODEL), F32)], axis=0)
    mod = _modulation(c16, w_ada, b_ada)
    mod5 = mod.reshape(DEPTH, MOD_ROWS, N_MOD, 1, D_MODEL)

    groups = []
    for x, row0 in ((x_prompt, 0), (x_sample, n_prompt)):
        batch, seq, _ = x.shape
        groups.append([x.reshape(batch * seq, D_MODEL), row0, batch, seq, _rope_tables(seq)])

    for layer in range(DEPTH):
        for grp in groups:
            x2d, row0, batch, seq, tabs = grp
            qt, k, vt, sga, pb = _mixer_in(layer, x2d, mod5, row0, seq, w, tabs)
            o = _attention(qt, k, vt, batch, seq)
            grp[0] = _mixer_out(layer, x2d, o, sga, pb, mod5, row0, seq, w)

    return tuple(grp[0].reshape(grp[2], grp[3], D_MODEL) for grp in groups)
```

```python
import math

import jax
import jax.numpy as jnp
import numpy as np
from jax import lax
from jax.experimental import pallas as pl
from jax.experimental.pallas import tpu as pltpu

D_MODEL = 1024
DEPTH = 4
N_HEADS = 8
QK_NOPE = 64
QK_ROPE = 32
QK_DIM = QK_NOPE + QK_ROPE
V_DIM = 64
Q_LORA = 384
KV_LORA = 256
ROPE_BASE = 10000.0
CHUNK = 128
SGU_GROUPS = 8
D_FF = 2816
N_MOD = 6
EPS = 1e-6

LANES = 128
SLOTS = N_HEADS * LANES
V_ROWS = 80
V_SLOTS = N_HEADS * V_ROWS
LAT_COLS = 768
MOD_ROWS = 16
VMEM_LIMIT = 56 * 1024 * 1024

TM_IN = 512
SUB_IN = 256
TM_OUT = 512
SUB_OUT = 256
TQ = 4096
ITEMS_PER_STEP = 32
TK = SUB_IN
TS = 256
STEPS_PER_BODY = 4
MOD_TN = 512

F32 = jnp.float32
BF16 = jnp.bfloat16


def _dot(a, b):
    return jnp.dot(a, b, preferred_element_type=F32)


def _rms_scale(x, n):
    return lax.rsqrt(jnp.sum(x * x, axis=-1, keepdims=True) * (1.0 / n) + EPS)


def _gelu_tanh(x):
    c = math.sqrt(2.0 / math.pi)
    hx = 0.5 * x
    return hx + hx * jnp.tanh(x * (c + (c * 0.044715) * (x * x)))


def _sigmoid(x):
    return 1.0 / (1.0 + jnp.exp(-x))


def _mod_kernel(c_ref, w_ref, b_ref, o_ref):
    c = c_ref[...]
    a = (c * _sigmoid(c)).astype(BF16)
    o_ref[...] = _dot(a, w_ref[...].astype(BF16)) + b_ref[...]


def _modulation(c16, w_ada, b_ada):
    n_cols = N_MOD * D_MODEL
    return pl.pallas_call(
        _mod_kernel,
        out_shape=jax.ShapeDtypeStruct((DEPTH, MOD_ROWS, n_cols), F32),
        grid=(DEPTH, n_cols // MOD_TN),
        in_specs=[
            pl.BlockSpec((MOD_ROWS, D_MODEL), lambda l, j: (0, 0)),
            pl.BlockSpec((None, D_MODEL, MOD_TN), lambda l, j: (l, 0, j)),
            pl.BlockSpec((None, 1, MOD_TN), lambda l, j: (l, 0, j)),
        ],
        out_specs=pl.BlockSpec((None, MOD_ROWS, MOD_TN), lambda l, j: (l, 0, j)),
        compiler_params=pltpu.CompilerParams(
            dimension_semantics=("arbitrary", "arbitrary"),
            vmem_limit_bytes=VMEM_LIMIT),
        name="adaln_modulation",
    )(c16, w_ada, b_ada.reshape(DEPTH, 1, n_cols))


def _norm_rope(x, a, b):
    return _rms_scale(x, QK_DIM) * (x * a + pltpu.roll(x, LANES // 2, axis=1) * b)


def _mixer_in_kernel(x_ref, sh_ref, sc_ref, n1g_ref, wall_ref, gqa_ref, gkva_ref,
                     wqt_ref, wknb_ref, wvt_ref, gqc_ref, gqrc_ref, gk_ref, gkr_ref, vone_ref,
                     tc_ref, ts_ref, tct_ref, tst_ref, gsgu_ref, ws_ref, bs_ref, wob_ref,
                     qt_ref, k_ref, vt_ref, sga_ref, pb_ref, prod_scr):
    for r in range(x_ref.shape[0] // SUB_IN):
        rows = slice(r * SUB_IN, (r + 1) * SUB_IN)
        _mixer_in_rows(x_ref.at[rows], sh_ref, sc_ref, n1g_ref, wall_ref, gqa_ref, gkva_ref,
                       wqt_ref, wknb_ref, wvt_ref, gqc_ref, gqrc_ref, gk_ref, gkr_ref, vone_ref,
                       tc_ref.at[rows], ts_ref.at[rows], tct_ref.at[:, rows], tst_ref.at[:, rows],
                       gsgu_ref, ws_ref, bs_ref, wob_ref,
                       qt_ref.at[:, rows], k_ref.at[rows], vt_ref.at[r], sga_ref.at[rows],
                       pb_ref.at[rows], prod_scr.at[rows])


def _mixer_in_rows(x_ref, sh_ref, sc_ref, n1g_ref, wall_ref, gqa_ref, gkva_ref,
                   wqt_ref, wknb_ref, wvt_ref, gqc_ref, gqrc_ref, gk_ref, gkr_ref, vone_ref,
                   tc_ref, ts_ref, tct_ref, tst_ref, gsgu_ref, ws_ref, bs_ref, wob_ref,
                   qt_ref, k_ref, vt_ref, sga_ref, pb_ref, prod_scr):
    tm = x_ref.shape[0]
    x = x_ref[...]
    h = (x * _rms_scale(x, D_MODEL)) * (n1g_ref[...] * (1.0 + sc_ref[...])) + sh_ref[...]
    hb = h.astype(BF16)

    wbig_ref = wall_ref.at[:, LAT_COLS:]
    lat = _dot(hb, wall_ref[:, :LAT_COLS])
    q_lat = lat[:, :Q_LORA]
    kv_lat = lat[:, Q_LORA:Q_LORA + KV_LORA]
    kr_slot = lat[:, Q_LORA + KV_LORA:]
    qn = (q_lat * _rms_scale(q_lat, Q_LORA) * gqa_ref[...]).astype(BF16)
    kvn = (kv_lat * _rms_scale(kv_lat, KV_LORA) * gkva_ref[...]).astype(BF16)
    nt = (((1,), (1,)), ((), ()))
    qt_all = lax.dot_general(wqt_ref[...], qn, nt, preferred_element_type=F32)
    kn_all = _dot(kvn, wknb_ref[...])
    vt = lax.dot_general(wvt_ref[...], kvn, nt, preferred_element_type=F32)
    vt_ref[...] = (vt + vone_ref[...]).astype(BF16)

    tc, ts = tc_ref[...], ts_ref[...]
    ka, kb = tc * gk_ref[...], ts * gkr_ref[...]
    qat, qbt = tct_ref[...] * gqc_ref[...], tst_ref[...] * gqrc_ref[...]

    def qk_heads(heads):
        for hd in heads:
            sl = slice(hd * LANES, (hd + 1) * LANES)
            xt = qt_all[sl, :]
            rt = lax.rsqrt(jnp.sum(xt * xt, axis=0, keepdims=True) * (1.0 / QK_DIM) + EPS)
            swapped = jnp.concatenate([xt[LANES // 2:], xt[:LANES // 2]], axis=0)
            qt_ref[sl, :] = (rt * (xt * qat + swapped * qbt)).astype(BF16)
            k_ref[:, sl] = _norm_rope(kn_all[:, sl] + kr_slot, ka, kb).astype(BF16)

    zu = _dot(hb, wbig_ref[:, 0:D_MODEL])
    qk_heads(range(0, 2))
    zv = _dot(hb, wbig_ref[:, D_MODEL:2 * D_MODEL])
    qk_heads(range(2, 4))
    u = _gelu_tanh(zu)
    za = _dot(hb, wbig_ref[:, 2 * D_MODEL:3 * D_MODEL])
    qk_heads(range(4, 6))
    gv = _gelu_tanh(zv)
    vn = (gv * _rms_scale(gv, D_MODEL) * gsgu_ref[...]).astype(BF16)
    zb = _dot(hb, wbig_ref[:, 3 * D_MODEL:4 * D_MODEL])
    qk_heads(range(6, 8))

    for c in range(0, tm // CHUNK, 2):
        rows0 = slice(c * CHUNK, (c + 1) * CHUNK)
        rows1 = slice((c + 1) * CHUNK, (c + 2) * CHUNK)
        for g in range(SGU_GROUPS):
            cols = slice(g * CHUNK, (g + 1) * CHUNK)
            pair = jnp.concatenate([vn[rows0, cols], vn[rows1, cols]], axis=1)
            mixed = _dot(ws_ref[g], pair)
            bias = bs_ref[:, cols]
            prod_scr[rows0, cols] = (u[rows0, cols] * (mixed[:, :CHUNK] + bias)).astype(BF16)
            prod_scr[rows1, cols] = (u[rows1, cols] * (mixed[:, CHUNK:] + bias)).astype(BF16)
    sga_ref[...] = _sigmoid(za)
    out_b = _dot(prod_scr[...], wob_ref[...])
    pb_ref[...] = _sigmoid(zb) * out_b


def _const_spec(shape, layer=None):
    if layer is None:
        return pl.BlockSpec(shape, lambda i: (0,) * len(shape), pipeline_mode=pl.Buffered(1))
    return pl.BlockSpec((None,) + shape, lambda i: (layer,) + (0,) * len(shape),
                        pipeline_mode=pl.Buffered(1))


def _mod_spec(layer, which, row0, tiles_per_seq):
    return pl.BlockSpec((None, None, None, 1, D_MODEL),
                        lambda i: (layer, row0 + i // tiles_per_seq, which, 0, 0))


def _mixer_in(layer, x2d, mod5, row0, seq, w, tabs):
    n_tok = x2d.shape[0]
    tm = TM_IN
    tps = seq // tm
    tile = lambda cols: pl.BlockSpec((tm, cols), lambda i: (i, 0))
    tab_spec = pl.BlockSpec((tm, LANES), lambda i: (i % tps, 0))
    tabt_spec = pl.BlockSpec((LANES, tm), lambda i: (0, i % tps))
    in_specs = [
        tile(D_MODEL),
        _mod_spec(layer, 0, row0, tps),
        _mod_spec(layer, 1, row0, tps),
        _const_spec((1, D_MODEL), layer),
        _const_spec((D_MODEL, LAT_COLS + 4 * D_MODEL), layer),
        _const_spec((1, Q_LORA), layer),
        _const_spec((1, KV_LORA), layer),
        _const_spec((SLOTS, Q_LORA), layer),
        _const_spec((KV_LORA, SLOTS), layer),
        _const_spec((V_SLOTS, KV_LORA), layer),
        _const_spec((LANES, 1), layer),
        _const_spec((LANES, 1), layer),
        _const_spec((1, LANES), layer),
        _const_spec((1, LANES), layer),
        _const_spec((V_SLOTS, SUB_IN)),
        tab_spec, tab_spec, tabt_spec, tabt_spec,
        _const_spec((1, D_MODEL), layer),
        _const_spec((SGU_GROUPS, CHUNK, CHUNK), layer),
        _const_spec((CHUNK, D_MODEL), layer),
        _const_spec((D_MODEL, D_MODEL), layer),
    ]
    out_shape = (
        jax.ShapeDtypeStruct((SLOTS, n_tok), BF16),
        jax.ShapeDtypeStruct((n_tok, SLOTS), BF16),
        jax.ShapeDtypeStruct((n_tok // SUB_IN, V_SLOTS, SUB_IN), BF16),
        jax.ShapeDtypeStruct((n_tok, D_MODEL), F32),
        jax.ShapeDtypeStruct((n_tok, D_MODEL), F32),
    )
    out_specs = (pl.BlockSpec((SLOTS, tm), lambda i: (0, i)), tile(SLOTS),
                 pl.BlockSpec((tm // SUB_IN, V_SLOTS, SUB_IN), lambda i: (i, 0, 0)),
                 tile(D_MODEL), tile(D_MODEL))
    return pl.pallas_call(
        _mixer_in_kernel,
        out_shape=out_shape,
        grid=(n_tok // tm,),
        in_specs=in_specs,
        out_specs=out_specs,
        scratch_shapes=[pltpu.VMEM((tm, D_MODEL), BF16)],
        compiler_params=pltpu.CompilerParams(
            dimension_semantics=("arbitrary",), vmem_limit_bytes=VMEM_LIMIT),
        name="mixer_in",
    )(x2d, mod5, mod5, w["norm1_g"], w["w_all"], w["q_a_norm_g"], w["kv_a_norm_g"],
      w["w_q_t"], w["w_kn_b"], w["w_v_t"], w["gq_col"], w["gq_rolled_col"], w["gk_slot"],
      w["gk_rolled"], w["v_ones"], tabs[0], tabs[1], tabs[2], tabs[3], w["sgu_norm_g"], w["w_s"],
      w["b_s_full"], w["w_o_b"])


def _attn_kernel(qt_ref, k_ref, vt_ref, o_ref, m_scr, acc_scr, *parity_bufs):
    n_heads, tq = qt_ref.shape[0] // LANES, qt_ref.shape[1]
    n_chunks = vt_ref.shape[0]
    m_scr[...] = jnp.full(m_scr.shape, -jnp.inf, F32)
    acc_scr[...] = jnp.zeros(acc_scr.shape, F32)
    st_scr, cm_scr = parity_bufs[0::2], parity_bufs[1::2]

    items = [(hh, sub) for hh in range(n_heads) for sub in range(tq // TS)]

    def scores(item, j, par):
        hh, sub = item
        sl = slice(hh * LANES, (hh + 1) * LANES)
        cs = slice(sub * TS, (sub + 1) * TS)
        kc = k_ref[pl.ds(pl.multiple_of(j * TK, TK), TK), sl]
        st = _dot(kc, qt_ref[sl, cs])
        st_scr[par][hh, :, cs] = st
        cm_scr[par][hh, :, cs] = jnp.max(st, axis=0, keepdims=True)

    def probs(item, par):
        hh, sub = item
        cs = slice(sub * TS, (sub + 1) * TS)
        m_old = m_scr[hh, :, cs]
        m_new = jnp.maximum(m_old, cm_scr[par][hh, :, cs])
        m_scr[hh, :, cs] = m_new
        pt = jnp.exp2(st_scr[par][hh, :, cs] - m_new).astype(BF16)
        return pt, jnp.exp2(m_old - m_new)

    def accumulate(item, j, pt, alpha):
        hh, sub = item
        rows = slice(hh * V_ROWS, (hh + 1) * V_ROWS)
        cs = slice(sub * TS, (sub + 1) * TS)
        acc_scr[hh, :, cs] = alpha * acc_scr[hh, :, cs] + _dot(vt_ref[j, rows, :], pt)

    def step(j, par, do_scores=True):
        for item in items:
            pt, alpha = probs(item, par)
            if do_scores:
                scores(item, j + 1, 1 - par)
            accumulate(item, j, pt, alpha)

    for item in items:
        scores(item, 0, 0)

    spb = STEPS_PER_BODY if n_chunks > 2 * STEPS_PER_BODY else 2

    def body(i, carry):
        for s in range(spb):
            step(spb * i + s, s % 2)
        return carry

    full = (n_chunks - 1) // spb
    lax.fori_loop(0, full, body, 0)
    for j in range(spb * full, n_chunks - 1):
        step(j, j % 2)
    step(n_chunks - 1, (n_chunks - 1) % 2, do_scores=False)
    outs = []
    for hh in range(n_heads):
        acc = acc_scr[hh]
        outs.append(acc[:V_DIM, :] * (1.0 / acc[V_DIM:V_DIM + 1, :]))
    o_ref[...] = jnp.concatenate(outs, axis=0).T.astype(BF16)


def _attention(qt, k, vt, batch, seq):
    n_tok = k.shape[0]
    tq = min(seq, TQ)
    n_qt = seq // tq
    nh = min(N_HEADS, max(2, ITEMS_PER_STEP * TS // tq))
    return pl.pallas_call(
        _attn_kernel,
        out_shape=jax.ShapeDtypeStruct((n_tok, N_HEADS * V_DIM), BF16),
        grid=(batch, N_HEADS // nh, n_qt),
        in_specs=[
            pl.BlockSpec((nh * LANES, tq), lambda b, hg, i: (hg, b * n_qt + i)),
            pl.BlockSpec((seq, nh * LANES), lambda b, hg, i: (b, hg)),
            pl.BlockSpec((seq // TK, nh * V_ROWS, TK), lambda b, hg, i: (b, hg, 0)),
        ],
        out_specs=pl.BlockSpec((tq, nh * V_DIM), lambda b, hg, i: (b * n_qt + i, hg)),
        scratch_shapes=[
            pltpu.VMEM((nh, 1, tq), F32),
            pltpu.VMEM((nh, V_ROWS, tq), F32),
        ] + 2 * [
            pltpu.VMEM((nh, TK, tq), F32),
            pltpu.VMEM((nh, 1, tq), F32),
        ],
        compiler_params=pltpu.CompilerParams(
            dimension_semantics=("arbitrary", "arbitrary", "arbitrary"),
            vmem_limit_bytes=VMEM_LIMIT),
        name="attention",
    )(qt, k, vt)


def _mixer_out_kernel(x_ref, o_ref, sga_ref, pb_ref, g1_ref, sh2_ref, sc2_ref, g2_ref,
                      woa_ref, wout_ref, n2g_ref, wfi_ref, wfo_ref, y_ref):
    for r in range(x_ref.shape[0] // SUB_OUT):
        rows = slice(r * SUB_OUT, (r + 1) * SUB_OUT)
        _mixer_out_rows(x_ref.at[rows], o_ref.at[rows], sga_ref.at[rows], pb_ref.at[rows],
                        g1_ref, sh2_ref, sc2_ref, g2_ref, woa_ref, wout_ref, n2g_ref, wfi_ref,
                        wfo_ref, y_ref.at[rows])


def _mixer_out_rows(x_ref, o_ref, sga_ref, pb_ref, g1_ref, sh2_ref, sc2_ref, g2_ref,
                    woa_ref, wout_ref, n2g_ref, wfi_ref, wfo_ref, y_ref):
    out_a = _dot(o_ref[...], woa_ref[...])
    merged = (sga_ref[...] * out_a + pb_ref[...]).astype(BF16)
    x1 = x_ref[...] + g1_ref[...] * _dot(merged, wout_ref[...])
    h2 = (x1 * _rms_scale(x1, D_MODEL)) * (n2g_ref[...] * (1.0 + sc2_ref[...])) + sh2_ref[...]
    h2 = h2.astype(BF16)
    up = _dot(h2, wfi_ref[:, :D_FF])
    gate = _dot(h2, wfi_ref[:, D_FF:])
    act = (gate * _sigmoid(gate) * up).astype(BF16)
    y_ref[...] = x1 + g2_ref[...] * _dot(act, wfo_ref[...])


def _mixer_out(layer, x2d, o, sga, pb, mod5, row0, seq, w):
    n_tok = x2d.shape[0]
    tm = TM_OUT
    tps = seq // tm
    tile = lambda cols: pl.BlockSpec((tm, cols), lambda i: (i, 0))
    in_specs = [
        tile(D_MODEL), tile(N_HEADS * V_DIM), tile(D_MODEL), tile(D_MODEL),
        _mod_spec(layer, 2, row0, tps),
        _mod_spec(layer, 3, row0, tps),
        _mod_spec(layer, 4, row0, tps),
        _mod_spec(layer, 5, row0, tps),
        _const_spec((N_HEADS * V_DIM, D_MODEL), layer),
        _const_spec((D_MODEL, D_MODEL), layer),
        _const_spec((1, D_MODEL), layer),
        _const_spec((D_MODEL, 2 * D_FF), layer),
        _const_spec((D_FF, D_MODEL), layer),
    ]
    return pl.pallas_call(
        _mixer_out_kernel,
        out_shape=jax.ShapeDtypeStruct((n_tok, D_MODEL), F32),
        grid=(n_tok // tm,),
        in_specs=in_specs,
        out_specs=tile(D_MODEL),
        compiler_params=pltpu.CompilerParams(
            dimension_semantics=("arbitrary",), vmem_limit_bytes=VMEM_LIMIT),
        name="mixer_out",
    )(x2d, o, sga, pb, mod5, mod5, mod5, mod5,
      w["w_o_a"], w["w_out"], w["norm2_g"], w["w_ffn_in"], w["w_ffn_out"])


def _slot_source():
    half = QK_ROPE // 2
    src = np.full((LANES,), QK_DIM, np.int32)
    src[0:48] = np.arange(0, 48)
    src[48:48 + half] = np.arange(QK_NOPE, QK_NOPE + half)
    src[64:80] = np.arange(48, QK_NOPE)
    src[LANES - half:] = np.arange(QK_NOPE + half, QK_DIM)
    return src


def _slot_lanes(a):
    a = jnp.pad(a, [(0, 0)] * (a.ndim - 1) + [(0, 1)])
    return jnp.take(a, _slot_source(), axis=-1)


def _head_slots(w, first_dim, width):
    L, K, _ = w.shape
    w = w.reshape(L, K, N_HEADS, width)
    w = jnp.pad(w, ((0, 0), (0, 0), (0, 0), (first_dim, QK_DIM - first_dim - width)))
    return _slot_lanes(w).reshape(L, K, SLOTS)


def _rope_tables(seq):
    pos = jnp.arange(seq, dtype=F32)
    inv = ROPE_BASE ** (-jnp.arange(0, QK_ROPE, 2, dtype=F32) / QK_ROPE)
    ang = pos[:, None] * inv[None, :]
    cos, sin = jnp.cos(ang), jnp.sin(ang)
    ones = jnp.ones((seq, QK_NOPE), F32)
    zn = jnp.zeros((seq, QK_NOPE), F32)
    tc = _slot_lanes(jnp.concatenate([ones, cos, cos], axis=1))
    ts = _slot_lanes(jnp.concatenate([zn, -sin, sin], axis=1))
    return tc, ts, tc.T, ts.T


def _prepare_weights(norm1_g, w_in, q_a_norm_g, kv_a_norm_g, w_q_b, w_kv_b, q_norm_g, k_norm_g,
                     w_o_a, sgu_norm_g, w_s, b_s, w_o_b, w_out, norm2_g, w_ffn_in, w_ffn_out):
    L = DEPTH
    lat_end = Q_LORA + KV_LORA + QK_ROPE
    w_kr = jnp.pad(w_in[:, :, Q_LORA + KV_LORA:lat_end], ((0, 0), (0, 0), (QK_NOPE, 0)))
    w_all = jnp.concatenate([w_in[:, :, :Q_LORA + KV_LORA].astype(BF16),
                             _slot_lanes(w_kr).astype(BF16),
                             w_in[:, :, lat_end:].astype(BF16)], axis=-1)
    kv4 = w_kv_b.reshape(L, KV_LORA, N_HEADS, QK_NOPE + V_DIM)
    w_kn = _head_slots(kv4[..., :QK_NOPE].reshape(L, KV_LORA, N_HEADS * QK_NOPE), 0, QK_NOPE)
    w_v = jnp.pad(kv4[..., QK_NOPE:], ((0, 0), (0, 0), (0, 0), (0, V_ROWS - V_DIM)))
    w_v = w_v.reshape(L, KV_LORA, V_SLOTS)
    v_ones = np.zeros((V_SLOTS, SUB_IN), np.float32)
    for hd in range(N_HEADS):
        v_ones[hd * V_ROWS + V_DIM, :] = 1.0
    q_scale = (QK_DIM ** -0.5) * math.log2(math.e)
    gq_slot = _slot_lanes(q_norm_g * q_scale).reshape(L, 1, LANES)
    gk_slot = _slot_lanes(k_norm_g).reshape(L, 1, LANES)
    return {
        "norm1_g": norm1_g.reshape(L, 1, D_MODEL),
        "w_all": w_all,
        "q_a_norm_g": q_a_norm_g.reshape(L, 1, Q_LORA),
        "kv_a_norm_g": kv_a_norm_g.reshape(L, 1, KV_LORA),
        "w_q_t": jnp.swapaxes(_head_slots(w_q_b, 0, QK_DIM), 1, 2).astype(BF16),
        "w_kn_b": w_kn.astype(BF16),
        "w_v_t": jnp.swapaxes(w_v, 1, 2).astype(BF16),
        "gq_col": jnp.swapaxes(gq_slot, 1, 2),
        "gq_rolled_col": jnp.swapaxes(jnp.roll(gq_slot, LANES // 2, axis=-1), 1, 2),
        "gk_slot": gk_slot,
        "gk_rolled": jnp.roll(gk_slot, LANES // 2, axis=-1),
        "v_ones": jnp.asarray(v_ones),
        "sgu_norm_g": sgu_norm_g.reshape(L, 1, D_MODEL),
        "w_s": w_s.astype(BF16),
        "b_s_full": jnp.repeat(jnp.swapaxes(b_s, 1, 2), CHUNK, axis=2),
        "w_o_b": w_o_b.astype(BF16),
        "w_o_a": w_o_a.astype(BF16),
        "w_out": w_out.astype(BF16),
        "norm2_g": norm2_g.reshape(L, 1, D_MODEL),
        "w_ffn_in": w_ffn_in.astype(BF16),
        "w_ffn_out": w_ffn_out.astype(BF16),
    }


def kernel(x_prompt, x_sample, c_prompt, c_sample, w_ada, b_ada, norm1_g, w_in, q_a_norm_g,
           kv_a_norm_g, w_q_b, w_kv_b, q_norm_g, k_norm_g, w_o_a, sgu_norm_g, w_s, b_s, w_o_b,
           w_out, norm2_g, w_ffn_in, w_ffn_out):
    w = _prepare_weights(norm1_g, w_in, q_a_norm_g, kv_a_norm_g, w_q_b, w_kv_b, q_norm_g,
                         k_norm_g, w_o_a, sgu_norm_g, w_s, b_s, w_o_b, w_out, norm2_g,
                         w_ffn_in, w_ffn_out)
    n_prompt, n_sample = c_prompt.shape[0], c_sample.shape[0]
    c16 = jnp.concatenate(
        [c_prompt, c_sample, jnp.zeros((MOD_ROWS - n_prompt - n_sample, D_MODEL), F32)], axis=0)
    mod = _modulation(c16, w_ada, b_ada)
    mod5 = mod.reshape(DEPTH, MOD_ROWS, N_MOD, 1, D_MODEL)

    groups = []
    for x, row0 in ((x_prompt, 0), (x_sample, n_prompt)):
        batch, seq, _ = x.shape
        groups.append([x.reshape(batch * seq, D_MODEL), row0, batch, seq, _rope_tables(seq)])

    for layer in range(DEPTH):
        for grp in groups:
            x2d, row0, batch, seq, tabs = grp
            qt, k, vt, sga, pb = _mixer_in(layer, x2d, mod5, row0, seq, w, tabs)
            o = _attention(qt, k, vt, batch, seq)
            grp[0] = _mixer_out(layer, x2d, o, sga, pb, mod5, row0, seq, w)

    return tuple(grp[0].reshape(grp[2], grp[3], D_MODEL) for grp in groups)
```

```python
import math

import jax
import jax.numpy as jnp
import numpy as np
from jax import lax
from jax.experimental import pallas as pl
from jax.experimental.pallas import tpu as pltpu

D_MODEL = 1024
DEPTH = 4
N_HEADS = 8
QK_NOPE = 64
QK_ROPE = 32
QK_DIM = QK_NOPE + QK_ROPE
V_DIM = 64
Q_LORA = 384
KV_LORA = 256
ROPE_BASE = 10000.0
CHUNK = 128
SGU_GROUPS = 8
D_FF = 2816
N_MOD = 6
EPS = 1e-6

LANES = 128
SLOTS = N_HEADS * LANES
V_ROWS = 80
V_SLOTS = N_HEADS * V_ROWS
LAT_COLS = 768
MOD_ROWS = 16
VMEM_LIMIT = 56 * 1024 * 1024

TM_IN = 512
SUB_IN = 256
TM_OUT = 512
SUB_OUT = 256
TQ = 4096
ITEMS_PER_STEP = 32
TK = SUB_IN
TS = 256
STEPS_PER_BODY = 4
MOD_TN = 2048

F32 = jnp.float32
BF16 = jnp.bfloat16


def _dot(a, b):
    return jnp.dot(a, b, preferred_element_type=F32)


def _rms_scale(x, n):
    return lax.rsqrt(jnp.sum(x * x, axis=-1, keepdims=True) * (1.0 / n) + EPS)


def _gelu_tanh(x):
    c = math.sqrt(2.0 / math.pi)
    hx = 0.5 * x
    return hx + hx * jnp.tanh(x * (c + (c * 0.044715) * (x * x)))


def _sigmoid(x):
    return 1.0 / (1.0 + jnp.exp(-x))


def _alternate(gens):
    while gens:
        gens = [g for g in gens if next(g, "done") != "done"]


def _mod_kernel(c_ref, w_ref, b_ref, o_ref):
    c = c_ref[...]
    a = (c * _sigmoid(c)).astype(BF16)
    o_ref[...] = _dot(a, w_ref[...].astype(BF16)) + b_ref[...]


def _modulation(c16, w_ada, b_ada):
    n_cols = N_MOD * D_MODEL
    return pl.pallas_call(
        _mod_kernel,
        out_shape=jax.ShapeDtypeStruct((DEPTH, MOD_ROWS, n_cols), F32),
        grid=(DEPTH, n_cols // MOD_TN),
        in_specs=[
            pl.BlockSpec((MOD_ROWS, D_MODEL), lambda l, j: (0, 0)),
            pl.BlockSpec((None, D_MODEL, MOD_TN), lambda l, j: (l, 0, j)),
            pl.BlockSpec((None, 1, MOD_TN), lambda l, j: (l, 0, j)),
        ],
        out_specs=pl.BlockSpec((None, MOD_ROWS, MOD_TN), lambda l, j: (l, 0, j)),
        compiler_params=pltpu.CompilerParams(
            dimension_semantics=("arbitrary", "arbitrary"),
            vmem_limit_bytes=VMEM_LIMIT),
        name="adaln_modulation",
    )(c16, w_ada, b_ada.reshape(DEPTH, 1, n_cols))


def _norm_rope(x, a, b):
    return _rms_scale(x, QK_DIM) * (x * a + pltpu.roll(x, LANES // 2, axis=1) * b)


def _mixer_in_kernel(x_ref, sh_ref, sc_ref, n1g_ref, wall_ref, gqa_ref, gkva_ref,
                     wqt_ref, wknb_ref, wvt_ref, gqc_ref, gqrc_ref, gk_ref, gkr_ref, vone_ref,
                     tc_ref, ts_ref, tct_ref, tst_ref, gsgu_ref, ws_ref, bs_ref, wob_ref,
                     qt_ref, k_ref, vt_ref, sga_ref, pb_ref, prod_scr):
    gens = []
    for r in range(x_ref.shape[0] // SUB_IN):
        rows = slice(r * SUB_IN, (r + 1) * SUB_IN)
        gens.append(_mixer_in_rows(
            x_ref.at[rows], sh_ref, sc_ref, n1g_ref, wall_ref, gqa_ref, gkva_ref,
            wqt_ref, wknb_ref, wvt_ref, gqc_ref, gqrc_ref, gk_ref, gkr_ref, vone_ref,
            tc_ref.at[rows], ts_ref.at[rows], tct_ref.at[:, rows], tst_ref.at[:, rows],
            gsgu_ref, ws_ref, bs_ref, wob_ref,
            qt_ref.at[:, rows], k_ref.at[rows], vt_ref.at[r], sga_ref.at[rows],
            pb_ref.at[rows], prod_scr.at[rows]))
    _alternate(gens)


def _mixer_in_rows(x_ref, sh_ref, sc_ref, n1g_ref, wall_ref, gqa_ref, gkva_ref,
                   wqt_ref, wknb_ref, wvt_ref, gqc_ref, gqrc_ref, gk_ref, gkr_ref, vone_ref,
                   tc_ref, ts_ref, tct_ref, tst_ref, gsgu_ref, ws_ref, bs_ref, wob_ref,
                   qt_ref, k_ref, vt_ref, sga_ref, pb_ref, prod_scr):
    tm = x_ref.shape[0]
    x = x_ref[...]
    h = (x * _rms_scale(x, D_MODEL)) * (n1g_ref[...] * (1.0 + sc_ref[...])) + sh_ref[...]
    hb = h.astype(BF16)
    yield

    wbig_ref = wall_ref.at[:, LAT_COLS:]
    lat = _dot(hb, wall_ref[:, :LAT_COLS])
    yield
    q_lat = lat[:, :Q_LORA]
    kv_lat = lat[:, Q_LORA:Q_LORA + KV_LORA]
    kr_slot = lat[:, Q_LORA + KV_LORA:]
    qn = (q_lat * _rms_scale(q_lat, Q_LORA) * gqa_ref[...]).astype(BF16)
    kvn = (kv_lat * _rms_scale(kv_lat, KV_LORA) * gkva_ref[...]).astype(BF16)
    nt = (((1,), (1,)), ((), ()))
    qt_all = lax.dot_general(wqt_ref[...], qn, nt, preferred_element_type=F32)
    kn_all = _dot(kvn, wknb_ref[...])
    vt = lax.dot_general(wvt_ref[...], kvn, nt, preferred_element_type=F32)
    vt_ref[...] = (vt + vone_ref[...]).astype(BF16)
    yield

    tc, ts = tc_ref[...], ts_ref[...]
    ka, kb = tc * gk_ref[...], ts * gkr_ref[...]
    qat, qbt = tct_ref[...] * gqc_ref[...], tst_ref[...] * gqrc_ref[...]

    def qk_heads(heads):
        for hd in heads:
            sl = slice(hd * LANES, (hd + 1) * LANES)
            xt = qt_all[sl, :]
            rt = lax.rsqrt(jnp.sum(xt * xt, axis=0, keepdims=True) * (1.0 / QK_DIM) + EPS)
            swapped = jnp.concatenate([xt[LANES // 2:], xt[:LANES // 2]], axis=0)
            qt_ref[sl, :] = (rt * (xt * qat + swapped * qbt)).astype(BF16)
            k_ref[:, sl] = _norm_rope(kn_all[:, sl] + kr_slot, ka, kb).astype(BF16)

    zu = _dot(hb, wbig_ref[:, 0:D_MODEL])
    yield
    qk_heads(range(0, 2))
    yield
    zv = _dot(hb, wbig_ref[:, D_MODEL:2 * D_MODEL])
    yield
    qk_heads(range(2, 4))
    u = _gelu_tanh(zu)
    yield
    za = _dot(hb, wbig_ref[:, 2 * D_MODEL:3 * D_MODEL])
    yield
    qk_heads(range(4, 6))
    gv = _gelu_tanh(zv)
    vn = (gv * _rms_scale(gv, D_MODEL) * gsgu_ref[...]).astype(BF16)
    yield
    zb = _dot(hb, wbig_ref[:, 3 * D_MODEL:4 * D_MODEL])
    yield
    qk_heads(range(6, 8))
    yield

    for c in range(0, tm // CHUNK, 2):
        rows0 = slice(c * CHUNK, (c + 1) * CHUNK)
        rows1 = slice((c + 1) * CHUNK, (c + 2) * CHUNK)
        for g in range(SGU_GROUPS):
            cols = slice(g * CHUNK, (g + 1) * CHUNK)
            pair = jnp.concatenate([vn[rows0, cols], vn[rows1, cols]], axis=1)
            mixed = _dot(ws_ref[g], pair)
            bias = bs_ref[:, cols]
            prod_scr[rows0, cols] = (u[rows0, cols] * (mixed[:, :CHUNK] + bias)).astype(BF16)
            prod_scr[rows1, cols] = (u[rows1, cols] * (mixed[:, CHUNK:] + bias)).astype(BF16)
            if g % 4 == 3:
                yield
    yield
    sga_ref[...] = _sigmoid(za)
    out_b = _dot(prod_scr[...], wob_ref[...])
    yield
    pb_ref[...] = _sigmoid(zb) * out_b


def _const_spec(shape, layer=None):
    if layer is None:
        return pl.BlockSpec(shape, lambda i: (0,) * len(shape), pipeline_mode=pl.Buffered(1))
    return pl.BlockSpec((None,) + shape, lambda i: (layer,) + (0,) * len(shape),
                        pipeline_mode=pl.Buffered(1))


def _mod_spec(layer, which, row0, tiles_per_seq):
    return pl.BlockSpec((None, None, None, 1, D_MODEL),
                        lambda i: (layer, row0 + i // tiles_per_seq, which, 0, 0))


def _mixer_in(layer, x2d, mod5, row0, seq, w, tabs):
    n_tok = x2d.shape[0]
    tm = TM_IN
    tps = seq // tm
    assert seq % tm == 0 and tm % SUB_IN == 0 and SUB_IN % (2 * CHUNK) == 0
    tile = lambda cols: pl.BlockSpec((tm, cols), lambda i: (i, 0))
    tab_spec = pl.BlockSpec((tm, LANES), lambda i: (i % tps, 0))
    tabt_spec = pl.BlockSpec((LANES, tm), lambda i: (0, i % tps))
    in_specs = [
        tile(D_MODEL),
        _mod_spec(layer, 0, row0, tps),
        _mod_spec(layer, 1, row0, tps),
        _const_spec((1, D_MODEL), layer),
        _const_spec((D_MODEL, LAT_COLS + 4 * D_MODEL), layer),
        _const_spec((1, Q_LORA), layer),
        _const_spec((1, KV_LORA), layer),
        _const_spec((SLOTS, Q_LORA), layer),
        _const_spec((KV_LORA, SLOTS), layer),
        _const_spec((V_SLOTS, KV_LORA), layer),
        _const_spec((LANES, 1), layer),
        _const_spec((LANES, 1), layer),
        _const_spec((1, LANES), layer),
        _const_spec((1, LANES), layer),
        _const_spec((V_SLOTS, SUB_IN)),
        tab_spec, tab_spec, tabt_spec, tabt_spec,
        _const_spec((1, D_MODEL), layer),
        _const_spec((SGU_GROUPS, CHUNK, CHUNK), layer),
        _const_spec((CHUNK, D_MODEL), layer),
        _const_spec((D_MODEL, D_MODEL), layer),
    ]
    out_shape = (
        jax.ShapeDtypeStruct((SLOTS, n_tok), BF16),
        jax.ShapeDtypeStruct((n_tok, SLOTS), BF16),
        jax.ShapeDtypeStruct((n_tok // SUB_IN, V_SLOTS, SUB_IN), BF16),
        jax.ShapeDtypeStruct((n_tok, D_MODEL), F32),
        jax.ShapeDtypeStruct((n_tok, D_MODEL), F32),
    )
    out_specs = (pl.BlockSpec((SLOTS, tm), lambda i: (0, i)), tile(SLOTS),
                 pl.BlockSpec((tm // SUB_IN, V_SLOTS, SUB_IN), lambda i: (i, 0, 0)),
                 tile(D_MODEL), tile(D_MODEL))
    return pl.pallas_call(
        _mixer_in_kernel,
        out_shape=out_shape,
        grid=(n_tok // tm,),
        in_specs=in_specs,
        out_specs=out_specs,
        scratch_shapes=[pltpu.VMEM((tm, D_MODEL), BF16)],
        compiler_params=pltpu.CompilerParams(
            dimension_semantics=("arbitrary",), vmem_limit_bytes=VMEM_LIMIT),
        name="mixer_in",
    )(x2d, mod5, mod5, w["norm1_g"], w["w_all"], w["q_a_norm_g"], w["kv_a_norm_g"],
      w["w_q_t"], w["w_kn_b"], w["w_v_t"], w["gq_col"], w["gq_rolled_col"], w["gk_slot"],
      w["gk_rolled"], w["v_ones"], tabs[0], tabs[1], tabs[2], tabs[3], w["sgu_norm_g"], w["w_s"],
      w["b_s_full"], w["w_o_b"])


def _attn_kernel(qt_ref, k_ref, vt_ref, o_ref, m_scr, acc_scr, *parity_bufs):
    n_heads, tq = qt_ref.shape[0] // LANES, qt_ref.shape[1]
    n_chunks = vt_ref.shape[0]
    m_scr[...] = jnp.full(m_scr.shape, -jnp.inf, F32)
    acc_scr[...] = jnp.zeros(acc_scr.shape, F32)
    st_scr, cm_scr = parity_bufs[0::2], parity_bufs[1::2]

    items = [(hh, sub) for hh in range(n_heads) for sub in range(tq // TS)]

    def scores(item, j, par):
        hh, sub = item
        sl = slice(hh * LANES, (hh + 1) * LANES)
        cs = slice(sub * TS, (sub + 1) * TS)
        kc = k_ref[pl.ds(pl.multiple_of(j * TK, TK), TK), sl]
        st = _dot(kc, qt_ref[sl, cs])
        st_scr[par][hh, :, cs] = st
        cm_scr[par][hh, :, cs] = jnp.max(st, axis=0, keepdims=True)

    def probs(item, par):
        hh, sub = item
        cs = slice(sub * TS, (sub + 1) * TS)
        m_old = m_scr[hh, :, cs]
        m_new = jnp.maximum(m_old, cm_scr[par][hh, :, cs])
        m_scr[hh, :, cs] = m_new
        pt = jnp.exp2(st_scr[par][hh, :, cs] - m_new).astype(BF16)
        return pt, jnp.exp2(m_old - m_new)

    def accumulate(item, j, pt, alpha):
        hh, sub = item
        rows = slice(hh * V_ROWS, (hh + 1) * V_ROWS)
        cs = slice(sub * TS, (sub + 1) * TS)
        acc_scr[hh, :, cs] = alpha * acc_scr[hh, :, cs] + _dot(vt_ref[j, rows, :], pt)

    def step(j, par, do_scores=True):
        for item in items:
            pt, alpha = probs(item, par)
            if do_scores:
                scores(item, j + 1, 1 - par)
            accumulate(item, j, pt, alpha)

    for item in items:
        scores(item, 0, 0)

    spb = STEPS_PER_BODY if n_chunks > 2 * STEPS_PER_BODY else 2

    def body(i, carry):
        for s in range(spb):
            step(spb * i + s, s % 2)
        return carry

    full = (n_chunks - 1) // spb
    lax.fori_loop(0, full, body, 0)
    for j in range(spb * full, n_chunks - 1):
        step(j, j % 2)
    step(n_chunks - 1, (n_chunks - 1) % 2, do_scores=False)
    outs = []
    for hh in range(n_heads):
        acc = acc_scr[hh]
        outs.append(acc[:V_DIM, :] * (1.0 / acc[V_DIM:V_DIM + 1, :]))
    o_ref[...] = jnp.concatenate(outs, axis=0).T.astype(BF16)


def _attention(qt, k, vt, batch, seq):
    n_tok = k.shape[0]
    tq = min(seq, TQ)
    n_qt = seq // tq
    nh = min(N_HEADS, max(2, ITEMS_PER_STEP * TS // tq))
    assert seq % tq == 0 and tq % TS == 0 and seq % (2 * TK) == 0 and N_HEADS % nh == 0
    return pl.pallas_call(
        _attn_kernel,
        out_shape=jax.ShapeDtypeStruct((n_tok, N_HEADS * V_DIM), BF16),
        grid=(batch, N_HEADS // nh, n_qt),
        in_specs=[
            pl.BlockSpec((nh * LANES, tq), lambda b, hg, i: (hg, b * n_qt + i)),
            pl.BlockSpec((seq, nh * LANES), lambda b, hg, i: (b, hg)),
            pl.BlockSpec((seq // TK, nh * V_ROWS, TK), lambda b, hg, i: (b, hg, 0)),
        ],
        out_specs=pl.BlockSpec((tq, nh * V_DIM), lambda b, hg, i: (b * n_qt + i, hg)),
        scratch_shapes=[
            pltpu.VMEM((nh, 1, tq), F32),
            pltpu.VMEM((nh, V_ROWS, tq), F32),
        ] + 2 * [
            pltpu.VMEM((nh, TK, tq), F32),
            pltpu.VMEM((nh, 1, tq), F32),
        ],
        compiler_params=pltpu.CompilerParams(
            dimension_semantics=("arbitrary", "arbitrary", "arbitrary"),
            vmem_limit_bytes=VMEM_LIMIT),
        name="attention",
    )(qt, k, vt)


def _mixer_out_kernel(x_ref, o_ref, sga_ref, pb_ref, g1_ref, sh2_ref, sc2_ref, g2_ref,
                      woa_ref, wout_ref, n2g_ref, wfi_ref, wfo_ref, y_ref):
    gens = []
    for r in range(x_ref.shape[0] // SUB_OUT):
        rows = slice(r * SUB_OUT, (r + 1) * SUB_OUT)
        gens.append(_mixer_out_rows(
            x_ref.at[rows], o_ref.at[rows], sga_ref.at[rows], pb_ref.at[rows], g1_ref, sh2_ref,
            sc2_ref, g2_ref, woa_ref, wout_ref, n2g_ref, wfi_ref, wfo_ref, y_ref.at[rows]))
    _alternate(gens)


def _mixer_out_rows(x_ref, o_ref, sga_ref, pb_ref, g1_ref, sh2_ref, sc2_ref, g2_ref,
                    woa_ref, wout_ref, n2g_ref, wfi_ref, wfo_ref, y_ref):
    out_a = _dot(o_ref[...], woa_ref[...])
    yield
    merged = (sga_ref[...] * out_a + pb_ref[...]).astype(BF16)
    x1 = x_ref[...] + g1_ref[...] * _dot(merged, wout_ref[...])
    yield
    h2 = (x1 * _rms_scale(x1, D_MODEL)) * (n2g_ref[...] * (1.0 + sc2_ref[...])) + sh2_ref[...]
    h2 = h2.astype(BF16)
    yield
    up = _dot(h2, wfi_ref[:, :D_FF])
    yield
    gate = _dot(h2, wfi_ref[:, D_FF:])
    yield
    act = (gate * _sigmoid(gate) * up).astype(BF16)
    yield
    y_ref[...] = x1 + g2_ref[...] * _dot(act, wfo_ref[...])


def _mixer_out(layer, x2d, o, sga, pb, mod5, row0, seq, w):
    n_tok = x2d.shape[0]
    tm = TM_OUT
    tps = seq // tm
    assert seq % tm == 0 and tm % SUB_OUT == 0
    tile = lambda cols: pl.BlockSpec((tm, cols), lambda i: (i, 0))
    in_specs = [
        tile(D_MODEL), tile(N_HEADS * V_DIM), tile(D_MODEL), tile(D_MODEL),
        _mod_spec(layer, 2, row0, tps),
        _mod_spec(layer, 3, row0, tps),
        _mod_spec(layer, 4, row0, tps),
        _mod_spec(layer, 5, row0, tps),
        _const_spec((N_HEADS * V_DIM, D_MODEL), layer),
        _const_spec((D_MODEL, D_MODEL), layer),
        _const_spec((1, D_MODEL), layer),
        _const_spec((D_MODEL, 2 * D_FF), layer),
        _const_spec((D_FF, D_MODEL), layer),
    ]
    return pl.pallas_call(
        _mixer_out_kernel,
        out_shape=jax.ShapeDtypeStruct((n_tok, D_MODEL), F32),
        grid=(n_tok // tm,),
        in_specs=in_specs,
        out_specs=tile(D_MODEL),
        compiler_params=pltpu.CompilerParams(
            dimension_semantics=("arbitrary",), vmem_limit_bytes=VMEM_LIMIT),
        name="mixer_out",
    )(x2d, o, sga, pb, mod5, mod5, mod5, mod5,
      w["w_o_a"], w["w_out"], w["norm2_g"], w["w_ffn_in"], w["w_ffn_out"])


def _slot_source():
    half = QK_ROPE // 2
    src = np.full((LANES,), QK_DIM, np.int32)
    src[0:48] = np.arange(0, 48)
    src[48:48 + half] = np.arange(QK_NOPE, QK_NOPE + half)
    src[64:80] = np.arange(48, QK_NOPE)
    src[LANES - half:] = np.arange(QK_NOPE + half, QK_DIM)
    return src


def _slot_lanes(a):
    a = jnp.pad(a, [(0, 0)] * (a.ndim - 1) + [(0, 1)])
    return jnp.take(a, _slot_source(), axis=-1)


def _head_slots(w, first_dim, width):
    L, K, _ = w.shape
    w = w.reshape(L, K, N_HEADS, width)
    w = jnp.pad(w, ((0, 0), (0, 0), (0, 0), (first_dim, QK_DIM - first_dim - width)))
    return _slot_lanes(w).reshape(L, K, SLOTS)


def _rope_tables(seq):
    pos = jnp.arange(seq, dtype=F32)
    inv = ROPE_BASE ** (-jnp.arange(0, QK_ROPE, 2, dtype=F32) / QK_ROPE)
    ang = pos[:, None] * inv[None, :]
    cos, sin = jnp.cos(ang), jnp.sin(ang)
    ones = jnp.ones((seq, QK_NOPE), F32)
    zn = jnp.zeros((seq, QK_NOPE), F32)
    tc = _slot_lanes(jnp.concatenate([ones, cos, cos], axis=1))
    ts = _slot_lanes(jnp.concatenate([zn, -sin, sin], axis=1))
    return tc, ts, tc.T, ts.T


def _prepare_weights(norm1_g, w_in, q_a_norm_g, kv_a_norm_g, w_q_b, w_kv_b, q_norm_g, k_norm_g,
                     w_o_a, sgu_norm_g, w_s, b_s, w_o_b, w_out, norm2_g, w_ffn_in, w_ffn_out):
    L = DEPTH
    lat_end = Q_LORA + KV_LORA + QK_ROPE
    w_kr = jnp.pad(w_in[:, :, Q_LORA + KV_LORA:lat_end], ((0, 0), (0, 0), (QK_NOPE, 0)))
    w_all = jnp.concatenate([w_in[:, :, :Q_LORA + KV_LORA].astype(BF16),
                             _slot_lanes(w_kr).astype(BF16),
                             w_in[:, :, lat_end:].astype(BF16)], axis=-1)
    kv4 = w_kv_b.reshape(L, KV_LORA, N_HEADS, QK_NOPE + V_DIM)
    w_kn = _head_slots(kv4[..., :QK_NOPE].reshape(L, KV_LORA, N_HEADS * QK_NOPE), 0, QK_NOPE)
    w_v = jnp.pad(kv4[..., QK_NOPE:], ((0, 0), (0, 0), (0, 0), (0, V_ROWS - V_DIM)))
    w_v = w_v.reshape(L, KV_LORA, V_SLOTS)
    v_ones = np.zeros((V_SLOTS, SUB_IN), np.float32)
    for hd in range(N_HEADS):
        v_ones[hd * V_ROWS + V_DIM, :] = 1.0
    q_scale = (QK_DIM ** -0.5) * math.log2(math.e)
    gq_slot = _slot_lanes(q_norm_g * q_scale).reshape(L, 1, LANES)
    gk_slot = _slot_lanes(k_norm_g).reshape(L, 1, LANES)
    return {
        "norm1_g": norm1_g.reshape(L, 1, D_MODEL),
        "w_all": w_all,
        "q_a_norm_g": q_a_norm_g.reshape(L, 1, Q_LORA),
        "kv_a_norm_g": kv_a_norm_g.reshape(L, 1, KV_LORA),
        "w_q_t": jnp.swapaxes(_head_slots(w_q_b, 0, QK_DIM), 1, 2).astype(BF16),
        "w_kn_b": w_kn.astype(BF16),
        "w_v_t": jnp.swapaxes(w_v, 1, 2).astype(BF16),
        "gq_col": jnp.swapaxes(gq_slot, 1, 2),
        "gq_rolled_col": jnp.swapaxes(jnp.roll(gq_slot, LANES // 2, axis=-1), 1, 2),
        "gk_slot": gk_slot,
        "gk_rolled": jnp.roll(gk_slot, LANES // 2, axis=-1),
        "v_ones": jnp.asarray(v_ones),
        "sgu_norm_g": sgu_norm_g.reshape(L, 1, D_MODEL),
        "w_s": w_s.astype(BF16),
        "b_s_full": jnp.repeat(jnp.swapaxes(b_s, 1, 2), CHUNK, axis=2),
        "w_o_b": w_o_b.astype(BF16),
        "w_o_a": w_o_a.astype(BF16),
        "w_out": w_out.astype(BF16),
        "norm2_g": norm2_g.reshape(L, 1, D_MODEL),
        "w_ffn_in": w_ffn_in.astype(BF16),
        "w_ffn_out": w_ffn_out.astype(BF16),
    }


def kernel(x_prompt, x_sample, c_prompt, c_sample, w_ada, b_ada, norm1_g, w_in, q_a_norm_g,
           kv_a_norm_g, w_q_b, w_kv_b, q_norm_g, k_norm_g, w_o_a, sgu_norm_g, w_s, b_s, w_o_b,
           w_out, norm2_g, w_ffn_in, w_ffn_out):
    w = _prepare_weights(norm1_g, w_in, q_a_norm_g, kv_a_norm_g, w_q_b, w_kv_b, q_norm_g,
                         k_norm_g, w_o_a, sgu_norm_g, w_s, b_s, w_o_b, w_out, norm2_g,
                         w_ffn_in, w_ffn_out)
    n_prompt, n_sample = c_prompt.shape[0], c_sample.shape[0]
    c16 = jnp.concatenate(
        [c_prompt, c_sample, jnp.zeros((MOD_ROWS - n_prompt - n_sample, D_MODEL), F32)], axis=0)
    mod = _modulation(c16, w_ada, b_ada)
    mod5 = mod.reshape(DEPTH, MOD_ROWS, N_MOD, 1, D_MODEL)

    groups = []
    for x, row0 in ((x_prompt, 0), (x_sample, n_prompt)):
        batch, seq, _ = x.shape
        groups.append([x.reshape(batch * seq, D_MODEL), row0, batch, seq, _rope_tables(seq)])

    for layer in range(DEPTH):
        for grp in groups:
            x2d, row0, batch, seq, tabs = grp
            qt, k, vt, sga, pb = _mixer_in(layer, x2d, mod5, row0, seq, w, tabs)
            o = _attention(qt, k, vt, batch, seq)
            grp[0] = _mixer_out(layer, x2d, o, sga, pb, mod5, row0, seq, w)

    return tuple(grp[0].reshape(grp[2], grp[3], D_MODEL) for grp in groups)
```

```python
import math

import jax
import jax.numpy as jnp
import numpy as np
from jax import lax
from jax.experimental import pallas as pl
from jax.experimental.pallas import tpu as pltpu

D_MODEL = 1024
DEPTH = 4
N_HEADS = 8
QK_NOPE = 64
QK_ROPE = 32
QK_DIM = QK_NOPE + QK_ROPE
V_DIM = 64
Q_LORA = 384
KV_LORA = 256
ROPE_BASE = 10000.0
CHUNK = 128
SGU_GROUPS = 8
D_FF = 2816
N_MOD = 6
EPS = 1e-6

LANES = 128
SLOTS = N_HEADS * LANES
V_ROWS = 80
V_SLOTS = N_HEADS * V_ROWS
LAT_COLS = 768
MOD_ROWS = 16
VMEM_LIMIT = 56 * 1024 * 1024

TM_IN = 512
SUB_IN = 256
TM_OUT = 512
SUB_OUT = 256
TQ = 4096
ITEMS_PER_STEP = 32
TK = SUB_IN
TS = 256
STEPS_PER_BODY = 4
MOD_TN = 2048

F32 = jnp.float32
BF16 = jnp.bfloat16


def _dot(a, b):
    return jnp.dot(a, b, preferred_element_type=F32)


def _rms_scale(x, n):
    return lax.rsqrt(jnp.sum(x * x, axis=-1, keepdims=True) * (1.0 / n) + EPS)


def _gelu_tanh(x):
    c = math.sqrt(2.0 / math.pi)
    hx = 0.5 * x
    return hx + hx * jnp.tanh(x * (c + (c * 0.044715) * (x * x)))


def _sigmoid(x):
    return 1.0 / (1.0 + jnp.exp(-x))


def _alternate(gens):
    while gens:
        gens = [g for g in gens if next(g, "done") != "done"]


def _mod_kernel(c_ref, w_ref, b_ref, o_ref):
    c = c_ref[...]
    a = (c * _sigmoid(c)).astype(BF16)
    o_ref[...] = _dot(a, w_ref[...].astype(BF16)) + b_ref[...]


def _modulation(c16, w_ada, b_ada):
    n_cols = N_MOD * D_MODEL
    return pl.pallas_call(
        _mod_kernel,
        out_shape=jax.ShapeDtypeStruct((DEPTH, MOD_ROWS, n_cols), F32),
        grid=(DEPTH, n_cols // MOD_TN),
        in_specs=[
            pl.BlockSpec((MOD_ROWS, D_MODEL), lambda l, j: (0, 0)),
            pl.BlockSpec((None, D_MODEL, MOD_TN), lambda l, j: (l, 0, j)),
            pl.BlockSpec((None, 1, MOD_TN), lambda l, j: (l, 0, j)),
        ],
        out_specs=pl.BlockSpec((None, MOD_ROWS, MOD_TN), lambda l, j: (l, 0, j)),
        compiler_params=pltpu.CompilerParams(
            dimension_semantics=("arbitrary", "arbitrary"),
            vmem_limit_bytes=VMEM_LIMIT),
        name="adaln_modulation",
    )(c16, w_ada, b_ada.reshape(DEPTH, 1, n_cols))


def _norm_rope(x, a, b):
    return _rms_scale(x, QK_DIM) * (x * a + pltpu.roll(x, LANES // 2, axis=1) * b)


def _mixer_in_kernel(x_ref, sh_ref, sc_ref, n1g_ref, wall_ref, gqa_ref, gkva_ref,
                     wqt_ref, wknb_ref, wvt_ref, gqc_ref, gqrc_ref, gk_ref, gkr_ref, vone_ref,
                     tc_ref, ts_ref, tct_ref, tst_ref, gsgu_ref, ws_ref, bs_ref, wob_ref,
                     qt_ref, k_ref, vt_ref, sga_ref, pb_ref, prod_scr):
    gens = []
    for r in range(x_ref.shape[0] // SUB_IN):
        rows = slice(r * SUB_IN, (r + 1) * SUB_IN)
        gens.append(_mixer_in_rows(
            x_ref.at[rows], sh_ref, sc_ref, n1g_ref, wall_ref, gqa_ref, gkva_ref,
            wqt_ref, wknb_ref, wvt_ref, gqc_ref, gqrc_ref, gk_ref, gkr_ref, vone_ref,
            tc_ref.at[rows], ts_ref.at[rows], tct_ref.at[:, rows], tst_ref.at[:, rows],
            gsgu_ref, ws_ref, bs_ref, wob_ref,
            qt_ref.at[:, rows], k_ref.at[rows], vt_ref.at[r], sga_ref.at[rows],
            pb_ref.at[rows], prod_scr.at[rows]))
    _alternate(gens)


def _mixer_in_rows(x_ref, sh_ref, sc_ref, n1g_ref, wall_ref, gqa_ref, gkva_ref,
                   wqt_ref, wknb_ref, wvt_ref, gqc_ref, gqrc_ref, gk_ref, gkr_ref, vone_ref,
                   tc_ref, ts_ref, tct_ref, tst_ref, gsgu_ref, ws_ref, bs_ref, wob_ref,
                   qt_ref, k_ref, vt_ref, sga_ref, pb_ref, prod_scr):
    tm = x_ref.shape[0]
    x = x_ref[...]
    h = (x * _rms_scale(x, D_MODEL)) * (n1g_ref[...] * (1.0 + sc_ref[...])) + sh_ref[...]
    hb = h.astype(BF16)
    yield

    wbig_ref = wall_ref.at[:, LAT_COLS:]
    lat = _dot(hb, wall_ref[:, :LAT_COLS])
    yield
    q_lat = lat[:, :Q_LORA]
    kv_lat = lat[:, Q_LORA:Q_LORA + KV_LORA]
    kr_slot = lat[:, Q_LORA + KV_LORA:]
    qn = (q_lat * _rms_scale(q_lat, Q_LORA) * gqa_ref[...]).astype(BF16)
    kvn = (kv_lat * _rms_scale(kv_lat, KV_LORA) * gkva_ref[...]).astype(BF16)
    nt = (((1,), (1,)), ((), ()))
    qt_all = lax.dot_general(wqt_ref[...], qn, nt, preferred_element_type=F32)
    kn_all = _dot(kvn, wknb_ref[...])
    vt = lax.dot_general(wvt_ref[...], kvn, nt, preferred_element_type=F32)
    vt_ref[...] = (vt + vone_ref[...]).astype(BF16)
    yield

    tc, ts = tc_ref[...], ts_ref[...]
    ka, kb = tc * gk_ref[...], ts * gkr_ref[...]
    qat, qbt = tct_ref[...] * gqc_ref[...], tst_ref[...] * gqrc_ref[...]

    def qk_heads(heads):
        for hd in heads:
            sl = slice(hd * LANES, (hd + 1) * LANES)
            xt = qt_all[sl, :]
            rt = lax.rsqrt(jnp.sum(xt * xt, axis=0, keepdims=True) * (1.0 / QK_DIM) + EPS)
            swapped = jnp.concatenate([xt[LANES // 2:], xt[:LANES // 2]], axis=0)
            qt_ref[sl, :] = (rt * (xt * qat + swapped * qbt)).astype(BF16)
            k_ref[:, sl] = _norm_rope(kn_all[:, sl] + kr_slot, ka, kb).astype(BF16)

    zu = _dot(hb, wbig_ref[:, 0:D_MODEL])
    yield
    qk_heads(range(0, 2))
    yield
    zv = _dot(hb, wbig_ref[:, D_MODEL:2 * D_MODEL])
    yield
    qk_heads(range(2, 4))
    u = _gelu_tanh(zu)
    yield
    za = _dot(hb, wbig_ref[:, 2 * D_MODEL:3 * D_MODEL])
    yield
    qk_heads(range(4, 6))
    gv = _gelu_tanh(zv)
    vn = (gv * _rms_scale(gv, D_MODEL) * gsgu_ref[...]).astype(BF16)
    yield
    zb = _dot(hb, wbig_ref[:, 3 * D_MODEL:4 * D_MODEL])
    yield
    qk_heads(range(6, 8))
    yield

    for c in range(0, tm // CHUNK, 2):
        rows0 = slice(c * CHUNK, (c + 1) * CHUNK)
        rows1 = slice((c + 1) * CHUNK, (c + 2) * CHUNK)
        for g in range(SGU_GROUPS):
            cols = slice(g * CHUNK, (g + 1) * CHUNK)
            pair = jnp.concatenate([vn[rows0, cols], vn[rows1, cols]], axis=1)
            mixed = _dot(ws_ref[g], pair)
            bias = bs_ref[:, cols]
            prod_scr[rows0, cols] = (u[rows0, cols] * (mixed[:, :CHUNK] + bias)).astype(BF16)
            prod_scr[rows1, cols] = (u[rows1, cols] * (mixed[:, CHUNK:] + bias)).astype(BF16)
            if g % 4 == 3:
                yield
    yield
    sga_ref[...] = _sigmoid(za)
    out_b = _dot(prod_scr[...], wob_ref[...])
    yield
    pb_ref[...] = _sigmoid(zb) * out_b


def _const_spec(shape, layer=None):
    if layer is None:
        return pl.BlockSpec(shape, lambda i: (0,) * len(shape), pipeline_mode=pl.Buffered(1))
    return pl.BlockSpec((None,) + shape, lambda i: (layer,) + (0,) * len(shape),
                        pipeline_mode=pl.Buffered(1))


def _mod_spec(layer, which, row0, tiles_per_seq):
    return pl.BlockSpec((None, None, None, 1, D_MODEL),
                        lambda i: (layer, row0 + i // tiles_per_seq, which, 0, 0))


def _mixer_in(layer, x2d, mod5, row0, seq, w, tabs):
    n_tok = x2d.shape[0]
    tm = TM_IN
    tps = seq // tm
    assert seq % tm == 0 and tm % SUB_IN == 0 and SUB_IN % (2 * CHUNK) == 0
    tile = lambda cols: pl.BlockSpec((tm, cols), lambda i: (i, 0))
    tab_spec = pl.BlockSpec((tm, LANES), lambda i: (i % tps, 0))
    tabt_spec = pl.BlockSpec((LANES, tm), lambda i: (0, i % tps))
    in_specs = [
        tile(D_MODEL),
        _mod_spec(layer, 0, row0, tps),
        _mod_spec(layer, 1, row0, tps),
        _const_spec((1, D_MODEL), layer),
        _const_spec((D_MODEL, LAT_COLS + 4 * D_MODEL), layer),
        _const_spec((1, Q_LORA), layer),
        _const_spec((1, KV_LORA), layer),
        _const_spec((SLOTS, Q_LORA), layer),
        _const_spec((KV_LORA, SLOTS), layer),
        _const_spec((V_SLOTS, KV_LORA), layer),
        _const_spec((LANES, 1), layer),
        _const_spec((LANES, 1), layer),
        _const_spec((1, LANES), layer),
        _const_spec((1, LANES), layer),
        _const_spec((V_SLOTS, SUB_IN)),
        tab_spec, tab_spec, tabt_spec, tabt_spec,
        _const_spec((1, D_MODEL), layer),
        _const_spec((SGU_GROUPS, CHUNK, CHUNK), layer),
        _const_spec((CHUNK, D_MODEL), layer),
        _const_spec((D_MODEL, D_MODEL), layer),
    ]
    out_shape = (
        jax.ShapeDtypeStruct((SLOTS, n_tok), BF16),
        jax.ShapeDtypeStruct((n_tok, SLOTS), BF16),
        jax.ShapeDtypeStruct((n_tok // SUB_IN, V_SLOTS, SUB_IN), BF16),
        jax.ShapeDtypeStruct((n_tok, D_MODEL), F32),
        jax.ShapeDtypeStruct((n_tok, D_MODEL), F32),
    )
    out_specs = (pl.BlockSpec((SLOTS, tm), lambda i: (0, i)), tile(SLOTS),
                 pl.BlockSpec((tm // SUB_IN, V_SLOTS, SUB_IN), lambda i: (i, 0, 0)),
                 tile(D_MODEL), tile(D_MODEL))
    return pl.pallas_call(
        _mixer_in_kernel,
        out_shape=out_shape,
        grid=(n_tok // tm,),
        in_specs=in_specs,
        out_specs=out_specs,
        scratch_shapes=[pltpu.VMEM((tm, D_MODEL), BF16)],
        compiler_params=pltpu.CompilerParams(
            dimension_semantics=("arbitrary",), vmem_limit_bytes=VMEM_LIMIT),
        name="mixer_in",
    )(x2d, mod5, mod5, w["norm1_g"], w["w_all"], w["q_a_norm_g"], w["kv_a_norm_g"],
      w["w_q_t"], w["w_kn_b"], w["w_v_t"], w["gq_col"], w["gq_rolled_col"], w["gk_slot"],
      w["gk_rolled"], w["v_ones"], tabs[0], tabs[1], tabs[2], tabs[3], w["sgu_norm_g"], w["w_s"],
      w["b_s_full"], w["w_o_b"])


def _attn_kernel(qt_ref, k_ref, vt_ref, o_ref, m_scr, acc_scr, *parity_bufs):
    n_heads, tq = qt_ref.shape[0] // LANES, qt_ref.shape[1]
    n_chunks = vt_ref.shape[0]
    m_scr[...] = jnp.full(m_scr.shape, -jnp.inf, F32)
    acc_scr[...] = jnp.zeros(acc_scr.shape, F32)
    st_scr, cm_scr = parity_bufs[0::2], parity_bufs[1::2]

    items = [(hh, sub) for hh in range(n_heads) for sub in range(tq // TS)]

    def scores(item, j, par):
        hh, sub = item
        sl = slice(hh * LANES, (hh + 1) * LANES)
        cs = slice(sub * TS, (sub + 1) * TS)
        kc = k_ref[pl.ds(pl.multiple_of(j * TK, TK), TK), sl]
        st = _dot(kc, qt_ref[sl, cs])
        st_scr[par][hh, :, cs] = st
        cm_scr[par][hh, :, cs] = jnp.max(st, axis=0, keepdims=True)

    def probs(item, par):
        hh, sub = item
        cs = slice(sub * TS, (sub + 1) * TS)
        m_old = m_scr[hh, :, cs]
        m_new = jnp.maximum(m_old, cm_scr[par][hh, :, cs])
        m_scr[hh, :, cs] = m_new
        pt = jnp.exp2(st_scr[par][hh, :, cs] - m_new).astype(BF16)
        return pt, jnp.exp2(m_old - m_new)

    def accumulate(item, j, pt, alpha):
        hh, sub = item
        rows = slice(hh * V_ROWS, (hh + 1) * V_ROWS)
        cs = slice(sub * TS, (sub + 1) * TS)
        acc_scr[hh, :, cs] = alpha * acc_scr[hh, :, cs] + _dot(vt_ref[j, rows, :], pt)

    def step(j, par, do_scores=True):
        for item in items:
            pt, alpha = probs(item, par)
            if do_scores:
                scores(item, j + 1, 1 - par)
            accumulate(item, j, pt, alpha)

    for item in items:
        scores(item, 0, 0)

    spb = STEPS_PER_BODY if n_chunks > 2 * STEPS_PER_BODY else 2

    def body(i, carry):
        for s in range(spb):
            step(spb * i + s, s % 2)
        return carry

    full = (n_chunks - 1) // spb
    lax.fori_loop(0, full, body, 0)
    for j in range(spb * full, n_chunks - 1):
        step(j, j % 2)
    step(n_chunks - 1, (n_chunks - 1) % 2, do_scores=False)
    outs = []
    for hh in range(n_heads):
        acc = acc_scr[hh]
        outs.append(acc[:V_DIM, :] * (1.0 / acc[V_DIM:V_DIM + 1, :]))
    o_ref[...] = jnp.concatenate(outs, axis=0).astype(BF16)


def _attention(qt, k, vt, batch, seq):
    n_tok = k.shape[0]
    tq = min(seq, TQ)
    n_qt = seq // tq
    nh = min(N_HEADS, max(2, ITEMS_PER_STEP * TS // tq))
    assert seq % tq == 0 and tq % TS == 0 and seq % (2 * TK) == 0 and N_HEADS % nh == 0
    return pl.pallas_call(
        _attn_kernel,
        out_shape=jax.ShapeDtypeStruct((N_HEADS * V_DIM, n_tok), BF16),
        grid=(batch, N_HEADS // nh, n_qt),
        in_specs=[
            pl.BlockSpec((nh * LANES, tq), lambda b, hg, i: (hg, b * n_qt + i)),
            pl.BlockSpec((seq, nh * LANES), lambda b, hg, i: (b, hg)),
            pl.BlockSpec((seq // TK, nh * V_ROWS, TK), lambda b, hg, i: (b, hg, 0)),
        ],
        out_specs=pl.BlockSpec((nh * V_DIM, tq), lambda b, hg, i: (hg, b * n_qt + i)),
        scratch_shapes=[
            pltpu.VMEM((nh, 1, tq), F32),
            pltpu.VMEM((nh, V_ROWS, tq), F32),
        ] + 2 * [
            pltpu.VMEM((nh, TK, tq), F32),
            pltpu.VMEM((nh, 1, tq), F32),
        ],
        compiler_params=pltpu.CompilerParams(
            dimension_semantics=("arbitrary", "arbitrary", "arbitrary"),
            vmem_limit_bytes=VMEM_LIMIT),
        name="attention",
    )(qt, k, vt)


def _mixer_out_kernel(x_ref, o_ref, sga_ref, pb_ref, g1_ref, sh2_ref, sc2_ref, g2_ref,
                      woa_ref, wout_ref, n2g_ref, wfi_ref, wfo_ref, y_ref):
    gens = []
    for r in range(x_ref.shape[0] // SUB_OUT):
        rows = slice(r * SUB_OUT, (r + 1) * SUB_OUT)
        gens.append(_mixer_out_rows(
            x_ref.at[rows], o_ref.at[:, rows], sga_ref.at[rows], pb_ref.at[rows], g1_ref, sh2_ref,
            sc2_ref, g2_ref, woa_ref, wout_ref, n2g_ref, wfi_ref, wfo_ref, y_ref.at[rows]))
    _alternate(gens)


def _mixer_out_rows(x_ref, o_ref, sga_ref, pb_ref, g1_ref, sh2_ref, sc2_ref, g2_ref,
                    woa_ref, wout_ref, n2g_ref, wfi_ref, wfo_ref, y_ref):
    out_a = lax.dot_general(o_ref[...], woa_ref[...], (((0,), (0,)), ((), ())),
                            preferred_element_type=F32)
    yield
    merged = (sga_ref[...] * out_a + pb_ref[...]).astype(BF16)
    x1 = x_ref[...] + g1_ref[...] * _dot(merged, wout_ref[...])
    yield
    h2 = (x1 * _rms_scale(x1, D_MODEL)) * (n2g_ref[...] * (1.0 + sc2_ref[...])) + sh2_ref[...]
    h2 = h2.astype(BF16)
    yield
    up = _dot(h2, wfi_ref[:, :D_FF])
    yield
    gate = _dot(h2, wfi_ref[:, D_FF:])
    yield
    act = (gate * _sigmoid(gate) * up).astype(BF16)
    yield
    y_ref[...] = x1 + g2_ref[...] * _dot(act, wfo_ref[...])


def _mixer_out(layer, x2d, o, sga, pb, mod5, row0, seq, w):
    n_tok = x2d.shape[0]
    tm = TM_OUT
    tps = seq // tm
    assert seq % tm == 0 and tm % SUB_OUT == 0
    tile = lambda cols: pl.BlockSpec((tm, cols), lambda i: (i, 0))
    in_specs = [
        tile(D_MODEL), pl.BlockSpec((N_HEADS * V_DIM, tm), lambda i: (0, i)),
        tile(D_MODEL), tile(D_MODEL),
        _mod_spec(layer, 2, row0, tps),
        _mod_spec(layer, 3, row0, tps),
        _mod_spec(layer, 4, row0, tps),
        _mod_spec(layer, 5, row0, tps),
        _const_spec((N_HEADS * V_DIM, D_MODEL), layer),
        _const_spec((D_MODEL, D_MODEL), layer),
        _const_spec((1, D_MODEL), layer),
        _const_spec((D_MODEL, 2 * D_FF), layer),
        _const_spec((D_FF, D_MODEL), layer),
    ]
    return pl.pallas_call(
        _mixer_out_kernel,
        out_shape=jax.ShapeDtypeStruct((n_tok, D_MODEL), F32),
        grid=(n_tok // tm,),
        in_specs=in_specs,
        out_specs=tile(D_MODEL),
        compiler_params=pltpu.CompilerParams(
            dimension_semantics=("arbitrary",), vmem_limit_bytes=VMEM_LIMIT),
        name="mixer_out",
    )(x2d, o, sga, pb, mod5, mod5, mod5, mod5,
      w["w_o_a"], w["w_out"], w["norm2_g"], w["w_ffn_in"], w["w_ffn_out"])


def _slot_source():
    half = QK_ROPE // 2
    src = np.full((LANES,), QK_DIM, np.int32)
    src[0:48] = np.arange(0, 48)
    src[48:48 + half] = np.arange(QK_NOPE, QK_NOPE + half)
    src[64:80] = np.arange(48, QK_NOPE)
    src[LANES - half:] = np.arange(QK_NOPE + half, QK_DIM)
    return src


def _slot_lanes(a):
    a = jnp.pad(a, [(0, 0)] * (a.ndim - 1) + [(0, 1)])
    return jnp.take(a, _slot_source(), axis=-1)


def _head_slots(w, first_dim, width):
    L, K, _ = w.shape
    w = w.reshape(L, K, N_HEADS, width)
    w = jnp.pad(w, ((0, 0), (0, 0), (0, 0), (first_dim, QK_DIM - first_dim - width)))
    return _slot_lanes(w).reshape(L, K, SLOTS)


def _rope_tables(seq):
    pos = jnp.arange(seq, dtype=F32)
    inv = ROPE_BASE ** (-jnp.arange(0, QK_ROPE, 2, dtype=F32) / QK_ROPE)
    ang = pos[:, None] * inv[None, :]
    cos, sin = jnp.cos(ang), jnp.sin(ang)
    ones = jnp.ones((seq, QK_NOPE), F32)
    zn = jnp.zeros((seq, QK_NOPE), F32)
    tc = _slot_lanes(jnp.concatenate([ones, cos, cos], axis=1))
    ts = _slot_lanes(jnp.concatenate([zn, -sin, sin], axis=1))
    return tc, ts, tc.T, ts.T


def _prepare_weights(norm1_g, w_in, q_a_norm_g, kv_a_norm_g, w_q_b, w_kv_b, q_norm_g, k_norm_g,
                     w_o_a, sgu_norm_g, w_s, b_s, w_o_b, w_out, norm2_g, w_ffn_in, w_ffn_out):
    L = DEPTH
    lat_end = Q_LORA + KV_LORA + QK_ROPE
    w_kr = jnp.pad(w_in[:, :, Q_LORA + KV_LORA:lat_end], ((0, 0), (0, 0), (QK_NOPE, 0)))
    w_all = jnp.concatenate([w_in[:, :, :Q_LORA + KV_LORA].astype(BF16),
                             _slot_lanes(w_kr).astype(BF16),
                             w_in[:, :, lat_end:].astype(BF16)], axis=-1)
    kv4 = w_kv_b.reshape(L, KV_LORA, N_HEADS, QK_NOPE + V_DIM)
    w_kn = _head_slots(kv4[..., :QK_NOPE].reshape(L, KV_LORA, N_HEADS * QK_NOPE), 0, QK_NOPE)
    w_v = jnp.pad(kv4[..., QK_NOPE:], ((0, 0), (0, 0), (0, 0), (0, V_ROWS - V_DIM)))
    w_v = w_v.reshape(L, KV_LORA, V_SLOTS)
    v_ones = np.zeros((V_SLOTS, SUB_IN), np.float32)
    for hd in range(N_HEADS):
        v_ones[hd * V_ROWS + V_DIM, :] = 1.0
    q_scale = (QK_DIM ** -0.5) * math.log2(math.e)
    gq_slot = _slot_lanes(q_norm_g * q_scale).reshape(L, 1, LANES)
    gk_slot = _slot_lanes(k_norm_g).reshape(L, 1, LANES)
    return {
        "norm1_g": norm1_g.reshape(L, 1, D_MODEL),
        "w_all": w_all,
        "q_a_norm_g": q_a_norm_g.reshape(L, 1, Q_LORA),
        "kv_a_norm_g": kv_a_norm_g.reshape(L, 1, KV_LORA),
        "w_q_t": jnp.swapaxes(_head_slots(w_q_b, 0, QK_DIM), 1, 2).astype(BF16),
        "w_kn_b": w_kn.astype(BF16),
        "w_v_t": jnp.swapaxes(w_v, 1, 2).astype(BF16),
        "gq_col": jnp.swapaxes(gq_slot, 1, 2),
        "gq_rolled_col": jnp.swapaxes(jnp.roll(gq_slot, LANES // 2, axis=-1), 1, 2),
        "gk_slot": gk_slot,
        "gk_rolled": jnp.roll(gk_slot, LANES // 2, axis=-1),
        "v_ones": jnp.asarray(v_ones),
        "sgu_norm_g": sgu_norm_g.reshape(L, 1, D_MODEL),
        "w_s": w_s.astype(BF16),
        "b_s_full": jnp.repeat(jnp.swapaxes(b_s, 1, 2), CHUNK, axis=2),
        "w_o_b": w_o_b.astype(BF16),
        "w_o_a": w_o_a.astype(BF16),
        "w_out": w_out.astype(BF16),
        "norm2_g": norm2_g.reshape(L, 1, D_MODEL),
        "w_ffn_in": w_ffn_in.astype(BF16),
        "w_ffn_out": w_ffn_out.astype(BF16),
    }


def kernel(x_prompt, x_sample, c_prompt, c_sample, w_ada, b_ada, norm1_g, w_in, q_a_norm_g,
           kv_a_norm_g, w_q_b, w_kv_b, q_norm_g, k_norm_g, w_o_a, sgu_norm_g, w_s, b_s, w_o_b,
           w_out, norm2_g, w_ffn_in, w_ffn_out):
    w = _prepare_weights(norm1_g, w_in, q_a_norm_g, kv_a_norm_g, w_q_b, w_kv_b, q_norm_g,
                         k_norm_g, w_o_a, sgu_norm_g, w_s, b_s, w_o_b, w_out, norm2_g,
                         w_ffn_in, w_ffn_out)
    n_prompt, n_sample = c_prompt.shape[0], c_sample.shape[0]
    c16 = jnp.concatenate(
        [c_prompt, c_sample, jnp.zeros((MOD_ROWS - n_prompt - n_sample, D_MODEL), F32)], axis=0)
    mod = _modulation(c16, w_ada, b_ada)
    mod5 = mod.reshape(DEPTH, MOD_ROWS, N_MOD, 1, D_MODEL)

    groups = []
    for x, row0 in ((x_prompt, 0), (x_sample, n_prompt)):
        batch, seq, _ = x.shape
        groups.append([x.reshape(batch * seq, D_MODEL), row0, batch, seq, _rope_tables(seq)])

    for layer in range(DEPTH):
        for grp in groups:
            x2d, row0, batch, seq, tabs = grp
            qt, k, vt, sga, pb = _mixer_in(layer, x2d, mod5, row0, seq, w, tabs)
            o = _attention(qt, k, vt, batch, seq)
            grp[0] = _mixer_out(layer, x2d, o, sga, pb, mod5, row0, seq, w)

    return tuple(grp[0].reshape(grp[2], grp[3], D_MODEL) for grp in groups)
```
